```python
import math
import jax, jax.numpy as jnp
from jax import lax
import numpy as np

D_MODEL = 1024
BATCH = 2
SEQ = 8192
DEPTH = 2

CHUNK = 64
Q_BLOCK = 128
F32 = jnp.float32

HEAD_DIM = 64
N_HEADS_SB = 8
N_HEADS_FOX = 8
FORGET_BIAS = 3.0

N_HEADS_DSA = 16
DSA_LATENT = 192
DSA_ROPE = 64
DSA_KEY_DIM = DSA_ROPE + DSA_LATENT
DSA_V_HEAD = 64
IDX_HEADS = 16
IDX_DIM = 64
IDX_TOPK = 256
IDX_SCALE = (IDX_HEADS * IDX_DIM) ** -0.5

ROPE_THETA = 500000.0
D_FF = 2816
CONV_WIDTH = 3
LN_EPS = 1e-5
ALPHA = (2 * DEPTH) ** 0.25
BETA = (8 * DEPTH) ** -0.25

W_IN_AB = 3 * (N_HEADS_SB + N_HEADS_FOX) * HEAD_DIM + N_HEADS_FOX
W_IN_C = N_HEADS_DSA * DSA_KEY_DIM + DSA_ROPE + DSA_LATENT + IDX_HEADS * IDX_DIM + IDX_DIM + IDX_HEADS

kernel_name = 'hybrid_stickbreak_fox_dsa_convffn'


def layer_norm(x, g, b):
    xf = x.astype(F32)
    mu = xf.mean(-1, keepdims=True)
    var = jnp.square(xf - mu).mean(-1, keepdims=True)
    y = (xf - mu) * lax.rsqrt(var + LN_EPS)
    return (y * g.astype(F32) + b.astype(F32)).astype(x.dtype)


def partial_rope(x, positions):
    d = x.shape[-1]
    rot = d // 4
    half = rot // 2
    freqs = ROPE_THETA ** (-jnp.arange(half, dtype=F32) / half)
    ang = positions.astype(F32)[..., None] * freqs
    ang = ang.reshape(ang.shape[:2] + (1,) * (x.ndim - 3) + (half,))
    cos, sin = jnp.cos(ang), jnp.sin(ang)
    xf = x.astype(F32)
    x1, x2, rest = xf[..., :half], xf[..., half:rot], xf[..., rot:]
    out = jnp.concatenate([x1 * cos - x2 * sin, x2 * cos + x1 * sin, rest], axis=-1)
    return out.astype(x.dtype)


def to_blocks(a):
    b, s = a.shape[:2]
    return a.reshape((b, s // Q_BLOCK, Q_BLOCK) + a.shape[2:]).swapaxes(0, 1)


def from_blocks(a):
    nb, b = a.shape[:2]
    return a.swapaxes(0, 1).reshape((b, nb * Q_BLOCK) + a.shape[3:])


def stick_breaking_attention(q, k, v):
    s_len, dh = q.shape[1], q.shape[3]
    scale = dh ** -0.5
    kf, vf = k.astype(F32), v.astype(F32)
    key_pos = jnp.arange(s_len)

    def block(args):
        i, qb = args
        qpos = i * Q_BLOCK + jnp.arange(Q_BLOCK)
        mask = key_pos[None, :] < qpos[:, None]
        z = jnp.einsum('bqhd,bshd->bhqs', qb.astype(F32), kf) * scale
        log_stay = jnp.where(mask, jax.nn.log_sigmoid(-z), 0.0)
        after = lax.cumsum(log_stay, axis=3, reverse=True) - log_stay
        w = jnp.where(mask, jnp.exp(jax.nn.log_sigmoid(z) + after), 0.0)
        return jnp.einsum('bhqs,bshd->bqhd', w, vf)

    out = lax.map(block, (jnp.arange(s_len // Q_BLOCK), to_blocks(q)))
    return from_blocks(out).astype(q.dtype)


def forgetting_attention(q, k, v, f_logit):
    s_len, dh = q.shape[1], q.shape[3]
    scale = dh ** -0.5
    kf, vf = k.astype(F32), v.astype(F32)
    cum = jnp.cumsum(jax.nn.log_sigmoid(f_logit.astype(F32)), axis=1)
    cum_k = cum.transpose(0, 2, 1)
    key_pos = jnp.arange(s_len)

    def block(args):
        i, qb, cq = args
        qpos = i * Q_BLOCK + jnp.arange(Q_BLOCK)
        mask = key_pos[None, :] <= qpos[:, None]
        logits = (jnp.einsum('bqhd,bshd->bhqs', qb.astype(F32), kf) * scale
                  + cq.transpose(0, 2, 1)[..., None] - cum_k[:, :, None, :])
        p = jax.nn.softmax(jnp.where(mask, logits, -jnp.inf), axis=-1)
        return jnp.einsum('bhqs,bshd->bqhd', p, vf)

    out = lax.map(block, (jnp.arange(s_len // Q_BLOCK), to_blocks(q), to_blocks(cum)))
    return from_blocks(out).astype(q.dtype)


def dsa_attention(q, k_lat, q_idx, k_idx, w_idx):
    s_len = q.shape[1]
    topk = min(IDX_TOPK, s_len // 4)
    scale = DSA_KEY_DIM ** -0.5
    klf, kif = k_lat.astype(F32), k_idx.astype(F32)
    key_pos = jnp.arange(s_len)

    def block(args):
        i, qb, qib, wb = args
        qpos = i * Q_BLOCK + jnp.arange(Q_BLOCK)
        visible = key_pos[None, :] < (qpos[:, None] // CHUNK + 1) * CHUNK
        idx_logits = jax.nn.relu(jnp.einsum('bqhd,bsd->bqhs', qib.astype(F32), kif))
        score = jnp.einsum('bqh,bqhs->bqs', wb.astype(F32), idx_logits) * IDX_SCALE
        score = jnp.where(visible[None], score, -jnp.inf)
        top_val, top_idx = lax.top_k(score, topk)
        valid = jnp.isfinite(top_val)
        k_sel = jax.vmap(lambda kb, ib: kb[ib])(klf, top_idx)
        logits = jnp.einsum('bqhd,bqkd->bqhk', qb.astype(F32), k_sel) * scale
        p = jax.nn.softmax(jnp.where(valid[:, :, None, :], logits, -jnp.inf), axis=-1)
        return jnp.einsum('bqhk,bqkc->bqhc', p, k_sel[..., DSA_ROPE:])

    out = lax.map(block, (jnp.arange(s_len // Q_BLOCK), to_blocks(q), to_blocks(q_idx), to_blocks(w_idx)))
    return from_blocks(out).astype(q.dtype)


def mixer_sb_fox(x, w_in, b_f, w_o):
    b, s, _ = x.shape
    gw = N_HEADS_SB * HEAD_DIM
    h = x @ w_in
    parts = jnp.split(h, [gw, 2 * gw, 3 * gw, 4 * gw, 5 * gw, 6 * gw], axis=-1)
    qa, ka, va = [p.reshape(b, s, N_HEADS_SB, HEAD_DIM) for p in parts[:3]]
    qb, kb, vb = [p.reshape(b, s, N_HEADS_FOX, HEAD_DIM) for p in parts[3:6]]
    f_logit = parts[6] + b_f
    oa = stick_breaking_attention(qa, ka, va).reshape(b, s, -1)
    ob = forgetting_attention(qb, kb, vb, f_logit).reshape(b, s, -1)
    return jnp.concatenate([oa, ob], axis=-1) @ w_o


def mixer_dsa(x, positions, w_in, w_uv, w_o):
    b, s, _ = x.shape
    h = x @ w_in
    c0 = N_HEADS_DSA * DSA_KEY_DIM
    c1 = c0 + DSA_ROPE
    c2 = c1 + DSA_LATENT
    c3 = c2 + IDX_HEADS * IDX_DIM
    c4 = c3 + IDX_DIM
    q, k_r, c_lat, q_idx, k_idx, w_idx = jnp.split(h, [c0, c1, c2, c3, c4], axis=-1)
    q = partial_rope(q.reshape(b, s, N_HEADS_DSA, DSA_KEY_DIM), positions)
    k_lat = partial_rope(jnp.concatenate([k_r, c_lat], axis=-1), positions)
    q_idx = partial_rope(q_idx.reshape(b, s, IDX_HEADS, IDX_DIM), positions)
    k_idx = partial_rope(k_idx, positions)
    ctx = dsa_attention(q, k_lat, q_idx, k_idx, w_idx)
    o = jnp.einsum('bshc,hcd->bshd', ctx, w_uv).reshape(b, s, -1)
    return o @ w_o


def conv_ffn(x, w_up, conv_w, conv_b, w_down):
    s = x.shape[1]
    gate, up = jnp.split(x @ w_up, 2, axis=-1)
    gp = jnp.pad(gate, ((0, 0), (CONV_WIDTH - 1, 0), (0, 0)))
    conv = conv_b + conv_w[0] * gp[:, 0:s]
    for j in range(1, CONV_WIDTH):
        conv = conv + conv_w[j] * gp[:, j:j + s]
    return (jax.nn.gelu(conv) * up) @ w_down


def setup_inputs(seed: int = 0) -> dict:
    key = jax.random.key(seed)
    ks = jax.random.split(key, 32)
    nrm = lambda k, shape, scale: jax.random.normal(k, shape, F32) * scale
    offsets = jax.random.randint(ks[1], (BATCH, 1), 0, 4096, dtype=jnp.int32)
    positions = (offsets + jnp.arange(SEQ, dtype=jnp.int32)[None, :]).astype(jnp.int32)
    inp = {'x': nrm(ks[0], (BATCH, SEQ, D_MODEL), 1.0), 'positions': positions}
    d_in = D_MODEL ** -0.5
    inp['l0_w_in'] = nrm(ks[2], (D_MODEL, W_IN_AB), d_in)
    inp['l0_b_f'] = FORGET_BIAS + nrm(ks[3], (N_HEADS_FOX,), 0.1)
    inp['l0_w_o'] = nrm(ks[4], (D_MODEL, D_MODEL), d_in * BETA)
    inp['l0_ln1_g'] = 1.0 + nrm(ks[5], (D_MODEL,), 0.02)
    inp['l0_ln1_b'] = nrm(ks[6], (D_MODEL,), 0.02)
    inp['l0_w_up'] = nrm(ks[7], (D_MODEL, 2 * D_FF), d_in)
    inp['l0_conv_w'] = nrm(ks[8], (CONV_WIDTH, D_FF), CONV_WIDTH ** -0.5)
    inp['l0_conv_b'] = nrm(ks[9], (D_FF,), 0.02)
    inp['l0_w_down'] = nrm(ks[10], (D_FF, D_MODEL), D_FF ** -0.5 * BETA)
    inp['l0_ln2_g'] = 1.0 + nrm(ks[11], (D_MODEL,), 0.02)
    inp['l0_ln2_b'] = nrm(ks[12], (D_MODEL,), 0.02)
    inp['l1_w_in'] = nrm(ks[13], (D_MODEL, W_IN_C), d_in)
    inp['l1_w_uv'] = nrm(ks[14], (N_HEADS_DSA, DSA_LATENT, DSA_V_HEAD), DSA_LATENT ** -0.5)
    inp['l1_w_o'] = nrm(ks[15], (N_HEADS_DSA * DSA_V_HEAD, D_MODEL), (N_HEADS_DSA * DSA_V_HEAD) ** -0.5 * BETA)
    inp['l1_ln1_g'] = 1.0 + nrm(ks[16], (D_MODEL,), 0.02)
    inp['l1_ln1_b'] = nrm(ks[17], (D_MODEL,), 0.02)
    inp['l1_w_up'] = nrm(ks[18], (D_MODEL, 2 * D_FF), d_in)
    inp['l1_conv_w'] = nrm(ks[19], (CONV_WIDTH, D_FF), CONV_WIDTH ** -0.5)
    inp['l1_conv_b'] = nrm(ks[20], (D_FF,), 0.02)
    inp['l1_w_down'] = nrm(ks[21], (D_FF, D_MODEL), D_FF ** -0.5 * BETA)
    inp['l1_ln2_g'] = 1.0 + nrm(ks[22], (D_MODEL,), 0.02)
    inp['l1_ln2_b'] = nrm(ks[23], (D_MODEL,), 0.02)
    return inp


def reference(x, positions,
              l0_w_in, l0_b_f, l0_w_o, l0_ln1_g, l0_ln1_b,
              l0_w_up, l0_conv_w, l0_conv_b, l0_w_down, l0_ln2_g, l0_ln2_b,
              l1_w_in, l1_w_uv, l1_w_o, l1_ln1_g, l1_ln1_b,
              l1_w_up, l1_conv_w, l1_conv_b, l1_w_down, l1_ln2_g, l1_ln2_b):
    mixer_params = [(l0_w_in, l0_b_f, l0_w_o), (l1_w_in, l1_w_uv, l1_w_o)]
    norm1 = [(l0_ln1_g, l0_ln1_b), (l1_ln1_g, l1_ln1_b)]
    ffn_params = [(l0_w_up, l0_conv_w, l0_conv_b, l0_w_down), (l1_w_up, l1_conv_w, l1_conv_b, l1_w_down)]
    norm2 = [(l0_ln2_g, l0_ln2_b), (l1_ln2_g, l1_ln2_b)]
    for i in range(DEPTH):
        if i % 2 == 0:
            m = mixer_sb_fox(x, *mixer_params[i])
        else:
            m = mixer_dsa(x, positions, *mixer_params[i])
        x = layer_norm(ALPHA * x + m, *norm1[i])
        x = layer_norm(ALPHA * x + conv_ffn(x, *ffn_params[i]), *norm2[i])
    return x
```

```python
import functools

import jax
import jax.numpy as jnp
from jax import lax
from jax.experimental import pallas as pl
from jax.experimental.pallas import tpu as pltpu

F32 = jnp.float32
BF16 = jnp.bfloat16

LANES = 128
VMEM_LIMIT = 56 * 1024 * 1024

HEAD_DIM = 64
N_HEADS_SB = 8
N_HEADS_FOX = 8
N_HEADS_DSA = 16
DSA_KEY_DIM = 256
DSA_ROPE = 64
DSA_V_HEAD = 64
IDX_HEADS = 16
IDX_DIM = 64
IDX_TOPK = 256
IDX_SCALE = (IDX_HEADS * IDX_DIM) ** -0.5
CHUNK = 64
ROPE_THETA = 500000.0
D_FF = 2816
LN_EPS = 1e-5
DEPTH = 2
ALPHA = (2 * DEPTH) ** 0.25
NEG = -1e30
INT_MIN = -(2 ** 31)


def _cparams(*sem):
    return pltpu.CompilerParams(dimension_semantics=sem, vmem_limit_bytes=VMEM_LIMIT)


def _dot(a, b):
    return jnp.dot(a, b, preferred_element_type=F32)


def _dot_nt(a, b):
    return lax.dot_general(a, b, (((1,), (1,)), ((), ())), preferred_element_type=F32)


def _split3(x):
    x1 = x.astype(BF16)
    r1 = x - x1.astype(F32)
    x2 = r1.astype(BF16)
    x3 = (r1 - x2.astype(F32)).astype(BF16)
    return x1, x2, x3


def _mm_kernel(x_ref, w_ref, o_ref):
    o_ref[...] = _dot(x_ref[...], w_ref[...]).astype(o_ref.dtype)


def _matmul(x, w, out_dtype, bm, bn, name):
    m, k = x.shape
    n = w.shape[1]
    return pl.pallas_call(
        _mm_kernel,
        grid=(m // bm, n // bn),
        in_specs=[pl.BlockSpec((bm, k), lambda i, j: (i, 0)),
                  pl.BlockSpec((k, bn), lambda i, j: (0, j))],
        out_specs=pl.BlockSpec((bm, bn), lambda i, j: (i, j)),
        out_shape=jax.ShapeDtypeStruct((m, n), out_dtype),
        compiler_params=_cparams("parallel", "parallel"),
        name=name,
    )(x, w)


def _layer_norm_rows(y, g, b):
    mu = jnp.mean(y, axis=-1, keepdims=True)
    yc = y - mu
    var = jnp.mean(yc * yc, axis=-1, keepdims=True)
    return yc * lax.rsqrt(var + LN_EPS) * g + b


def _mm_ln_kernel(*refs, n_pairs):
    xs = refs[:n_pairs]
    ws = refs[n_pairs:2 * n_pairs]
    res_ref, g_ref, b_ref, of_ref, ob_ref = refs[2 * n_pairs:]
    acc = _dot(xs[0][...], ws[0][...])
    for p in range(1, n_pairs):
        acc = acc + _dot(xs[p][...], ws[p][...])
    out = _layer_norm_rows(ALPHA * res_ref[...] + acc, g_ref[...], b_ref[...])
    of_ref[...] = out
    ob_ref[...] = out.astype(BF16)


def _matmul_ln(pairs, res, g, b, bm, name):
    m, d = res.shape
    n_pairs = len(pairs)
    in_specs = ([pl.BlockSpec((bm, x.shape[1]), lambda i: (i, 0)) for x, _ in pairs]
                + [pl.BlockSpec(w.shape, lambda i: (0, 0)) for _, w in pairs]
                + [pl.BlockSpec((bm, d), lambda i: (i, 0)),
                   pl.BlockSpec((1, d), lambda i: (0, 0)),
                   pl.BlockSpec((1, d), lambda i: (0, 0))])
    return pl.pallas_call(
        functools.partial(_mm_ln_kernel, n_pairs=n_pairs),
        grid=(m // bm,),
        in_specs=in_specs,
        out_specs=[pl.BlockSpec((bm, d), lambda i: (i, 0)),
                   pl.BlockSpec((bm, d), lambda i: (i, 0))],
        out_shape=[jax.ShapeDtypeStruct((m, d), F32), jax.ShapeDtypeStruct((m, d), BF16)],
        compiler_params=_cparams("parallel"),
        name=name,
    )(*[x for x, _ in pairs], *[w for _, w in pairs], res, g.reshape(1, d), b.reshape(1, d))


def _shift_rows(g, halo, k, row):
    out = pltpu.roll(g, k, 0)
    for r in range(k):
        out = jnp.where(row == r, halo[8 - k + r:8 - k + r + 1, :], out)
    return out


def _ffn_down_kernel(g_ref, gh_ref, u_ref, cw_ref, cb_ref, wd_ref, res_ref, lg_ref, lb_ref,
                     of_ref, ob_ref, *, bm, seq):
    i = pl.program_id(0)
    g = g_ref[...].astype(F32)
    starts_seq = lax.rem(i * bm, seq) == 0
    halo = jnp.where(starts_seq, 0.0, gh_ref[...].astype(F32))
    row = lax.broadcasted_iota(jnp.int32, g.shape, 0)
    g1 = _shift_rows(g, halo, 1, row)
    g2 = _shift_rows(g, halo, 2, row)
    cw = cw_ref[...]
    conv = cb_ref[...] + cw[0:1, :] * g2
    conv = conv + cw[1:2, :] * g1
    conv = conv + cw[2:3, :] * g
    cdf = 0.5 * (1.0 + jnp.tanh(0.7978845608028654 * (conv + 0.044715 * (conv * conv * conv))))
    act = (conv * cdf) * u_ref[...].astype(F32)
    acc = _dot(act.astype(BF16), wd_ref[...])
    out = _layer_norm_rows(ALPHA * res_ref[...] + acc, lg_ref[...], lb_ref[...])
    of_ref[...] = out
    ob_ref[...] = out.astype(BF16)


def _ffn_down(u, conv_w, conv_b, w_down, res, g, b, seq, bm, name):
    m, d = res.shape
    f = w_down.shape[0]
    hb = bm // 8
    return pl.pallas_call(
        functools.partial(_ffn_down_kernel, bm=bm, seq=seq),
        grid=(m // bm,),
        in_specs=[pl.BlockSpec((bm, f), lambda i: (i, 0)),
                  pl.BlockSpec((8, f), lambda i: (jnp.maximum(i * hb - 1, 0), 0)),
                  pl.BlockSpec((bm, f), lambda i: (i, 1)),
                  pl.BlockSpec((3, f), lambda i: (0, 0)),
                  pl.BlockSpec((1, f), lambda i: (0, 0)),
                  pl.BlockSpec((f, d), lambda i: (0, 0)),
                  pl.BlockSpec((bm, d), lambda i: (i, 0)),
                  pl.BlockSpec((1, d), lambda i: (0, 0)),
                  pl.BlockSpec((1, d), lambda i: (0, 0))],
        out_specs=[pl.BlockSpec((bm, d), lambda i: (i, 0)),
                   pl.BlockSpec((bm, d), lambda i: (i, 0))],
        out_shape=[jax.ShapeDtypeStruct((m, d), F32), jax.ShapeDtypeStruct((m, d), BF16)],
        compiler_params=_cparams("parallel"),
        name=name,
    )(u, u, u, conv_w, conv_b.reshape(1, f), w_down, res, g.reshape(1, d), b.reshape(1, d))


def _cum_kernel(f_ref, bias_ref, o_ref, *, rows_per_seq):
    x = f_ref[...] + bias_ref[...]
    ls = jnp.minimum(x, 0.0) - jnp.log(1.0 + jnp.exp(-jnp.abs(x)))
    r = x.shape[0]
    incl = (lax.broadcasted_iota(jnp.int32, (LANES, LANES), 0)
            <= lax.broadcasted_iota(jnp.int32, (LANES, LANES), 1)).astype(BF16)
    a1, a2, a3 = _split3(ls)
    cs = _dot(a1, incl) + _dot(a2, incl) + _dot(a3, incl)
    tot = jnp.broadcast_to(cs[:, LANES - 1:LANES], (r, LANES))
    ri = lax.broadcasted_iota(jnp.int32, (r, r), 0)
    ci = lax.broadcasted_iota(jnp.int32, (r, r), 1)
    shift = rows_per_seq.bit_length() - 1
    same_seq = (ri >> shift) == (ci >> shift)
    before = jnp.where(same_seq & (ci < ri), 1.0, 0.0).astype(BF16)
    t1, t2, t3 = _split3(tot)
    off = _dot(before, t1) + _dot(before, t2) + _dot(before, t3)
    o_ref[...] = cs + off


def _forget_cumsum(fl, b_f, batch, seq):
    nh = N_HEADS_FOX
    f = fl[:, :nh].reshape(batch, seq, nh).transpose(0, 2, 1).reshape(batch * nh * (seq // LANES), LANES)
    bias = jnp.broadcast_to(jnp.tile(b_f.astype(F32), batch)[:, None, None],
                            (batch * nh, seq // LANES, LANES)).reshape(f.shape)
    cum = pl.pallas_call(
        functools.partial(_cum_kernel, rows_per_seq=seq // LANES),
        out_shape=jax.ShapeDtypeStruct(f.shape, F32),
        compiler_params=pltpu.CompilerParams(vmem_limit_bytes=VMEM_LIMIT),
        name="forget_cumsum",
    )(f, bias)
    return cum.reshape(batch * nh, seq)


def _softplus(z):
    return jnp.maximum(z, 0.0) + jnp.log(1.0 + jnp.exp(-jnp.abs(z)))


def _sb_kernel(q_ref, k_ref, v_ref, o_ref, *, tq):
    i = pl.program_id(2)
    q = q_ref[...].astype(F32) * (HEAD_DIM ** -0.5)
    lane = lax.broadcasted_iota(jnp.int32, (1, LANES), 1)
    row = lax.broadcasted_iota(jnp.int32, (tq, tq), 0)
    col = lax.broadcasted_iota(jnp.int32, (tq, tq), 1)
    strict = col < row
    suffix = (row >= col).astype(BF16)
    accs = []
    for hh in range(2):
        qm = jnp.where((lane >> 6) == hh, q, 0.0).astype(BF16)

        def tile(kb, carry, diag, qm=qm):
            c, acc = carry
            start = pl.multiple_of(kb * tq, tq)
            k = k_ref[pl.ds(start, tq), :]
            v = v_ref[pl.ds(start, tq), :]
            z = _dot_nt(qm, k)
            ls = -_softplus(z)
            if diag:
                ls = jnp.where(strict, ls, 0.0)
            hi = ls.astype(BF16)
            lo = (ls - hi.astype(F32)).astype(BF16)
            cs = _dot(hi, suffix) + _dot(lo, suffix)
            w = jnp.exp(z + cs + c)
            if diag:
                w = jnp.where(strict, w, 0.0)
            acc = acc + _dot(w.astype(BF16), v)
            return c + cs[:, 0:1], acc

        carry = (jnp.zeros((tq, 1), F32), jnp.zeros((tq, LANES), F32))
        carry = tile(i, carry, True)
        carry = lax.fori_loop(0, i, lambda j, cr: tile(i - 1 - j, cr, False), carry)
        accs.append(carry[1])
    o_ref[...] = jnp.where(lane < HEAD_DIM, accs[0], accs[1]).astype(o_ref.dtype)


def _sb_attention(h0, batch, seq, tq):
    npair = N_HEADS_SB // 2
    nq = seq // tq
    q_off, k_off, v_off = 0, npair, 2 * npair
    return pl.pallas_call(
        functools.partial(_sb_kernel, tq=tq),
        grid=(batch, npair, nq),
        in_specs=[pl.BlockSpec((tq, LANES), lambda b, p, i: (b * nq + i, q_off + p)),
                  pl.BlockSpec((seq, LANES), lambda b, p, i: (b, k_off + p)),
                  pl.BlockSpec((seq, LANES), lambda b, p, i: (b, v_off + p))],
        out_specs=pl.BlockSpec((tq, LANES), lambda b, p, i: (b * nq + i, p)),
        out_shape=jax.ShapeDtypeStruct((batch * seq, N_HEADS_SB * HEAD_DIM), BF16),
        compiler_params=_cparams("parallel", "parallel", "arbitrary"),
        name="sb_attention",
    )(h0, h0, h0)


def _fox_kernel(q_ref, k_ref, v_ref, cq_ref, ck_ref, o_ref, *, tq):
    i = pl.program_id(2)
    q = q_ref[...].astype(F32) * (HEAD_DIM ** -0.5)
    lane = lax.broadcasted_iota(jnp.int32, (1, LANES), 1)
    row = lax.broadcasted_iota(jnp.int32, (tq, tq), 0)
    col = lax.broadcasted_iota(jnp.int32, (tq, tq), 1)
    causal = col <= row
    outs = []
    for hh in range(2):
        qm = jnp.where((lane >> 6) == hh, q, 0.0).astype(BF16)
        cq = cq_ref[hh]

        def tile(kb, carry, diag, qm=qm, cq=cq, hh=hh):
            m, l, acc = carry
            start = pl.multiple_of(kb * tq, tq)
            k = k_ref[pl.ds(start, tq), :]
            v = v_ref[pl.ds(start, tq), :]
            ck = ck_ref[hh, :, pl.ds(start, tq)]
            lg = (_dot_nt(qm, k) + cq) - ck
            if diag:
                lg = jnp.where(causal, lg, NEG)
            m_new = jnp.maximum(m, jnp.max(lg, axis=1, keepdims=True))
            p = jnp.exp(lg - m_new)
            a = jnp.exp(m - m_new)
            l = a * l + jnp.sum(p, axis=1, keepdims=True)
            acc = a * acc + _dot(p.astype(BF16), v)
            return m_new, l, acc

        carry = (jnp.full((tq, 1), NEG, F32), jnp.zeros((tq, 1), F32), jnp.zeros((tq, LANES), F32))
        carry = tile(i, carry, True)
        carry = lax.fori_loop(0, i, lambda j, cr: tile(i - 1 - j, cr, False), carry)
        outs.append(carry[2] / carry[1])
    o_ref[...] = jnp.where(lane < HEAD_DIM, outs[0], outs[1]).astype(o_ref.dtype)


def _fox_attention(h0, cum, batch, seq, tq):
    npair = N_HEADS_FOX // 2
    nq = seq // tq
    base = 3 * (N_HEADS_SB // 2)
    q_off, k_off, v_off = base, base + npair, base + 2 * npair
    cum_q = cum.reshape(batch * N_HEADS_FOX, seq, 1)
    cum_k = cum.reshape(batch * N_HEADS_FOX, 1, seq)
    return pl.pallas_call(
        functools.partial(_fox_kernel, tq=tq),
        grid=(batch, npair, nq),
        in_specs=[pl.BlockSpec((tq, LANES), lambda b, p, i: (b * nq + i, q_off + p)),
                  pl.BlockSpec((seq, LANES), lambda b, p, i: (b, k_off + p)),
                  pl.BlockSpec((seq, LANES), lambda b, p, i: (b, v_off + p)),
                  pl.BlockSpec((2, tq, 1), lambda b, p, i: (b * npair + p, i, 0)),
                  pl.BlockSpec((2, 1, seq), lambda b, p, i: (b * npair + p, 0, 0))],
        out_specs=pl.BlockSpec((tq, LANES), lambda b, p, i: (b * nq + i, p)),
        out_shape=jax.ShapeDtypeStruct((batch * seq, N_HEADS_FOX * HEAD_DIM), BF16),
        compiler_params=_cparams("parallel", "parallel", "arbitrary"),
        name="fox_attention",
    )(h0, h0, h0, cum_q, cum_k)


def _rope_q(x, cos_t, sin_t, lane):
    xs = jnp.where(lane < DSA_ROPE // 2, pltpu.roll(x, LANES - DSA_ROPE // 2, 1), pltpu.roll(x, DSA_ROPE // 2, 1))
    return x * cos_t + xs * sin_t


def _rope_idx(x, cos_t, sin_t, lane):
    half = IDX_DIM // 8
    xs = jnp.where((lane & (IDX_DIM - 1)) < half, pltpu.roll(x, LANES - half, 1), pltpu.roll(x, half, 1))
    return x * cos_t + xs * sin_t


def _dsa_prep_kernel(kl_ref, km_ref, cq_ref, sq_ref, ci_ref, si_ref, klo_ref, kio_ref):
    lane = lax.broadcasted_iota(jnp.int32, (1, LANES), 1)
    klo_ref[:, 0:LANES] = _rope_q(kl_ref[:, 0:LANES], cq_ref[...], sq_ref[...], lane).astype(BF16)
    klo_ref[:, LANES:2 * LANES] = kl_ref[:, LANES:2 * LANES].astype(BF16)
    yr = _rope_idx(km_ref[...], ci_ref[...], si_ref[...], lane)
    kio_ref[...] = jnp.where(lane < IDX_DIM, yr, pltpu.roll(yr, IDX_DIM, 1)).astype(BF16)


def _dsa_prep(hr, tabs, rb):
    m = hr.shape[0]
    kl_blk = IDX_HEADS * IDX_DIM // (2 * LANES)
    km_blk = (IDX_HEADS * IDX_DIM + DSA_KEY_DIM) // LANES
    tspec = pl.BlockSpec((rb, LANES), lambda i: (i, 0))
    return pl.pallas_call(
        _dsa_prep_kernel,
        grid=(m // rb,),
        in_specs=[pl.BlockSpec((rb, 2 * LANES), lambda i: (i, kl_blk)),
                  pl.BlockSpec((rb, LANES), lambda i: (i, km_blk)),
                  tspec, tspec, tspec, tspec],
        out_specs=[pl.BlockSpec((rb, 2 * LANES), lambda i: (i, 0)),
                   pl.BlockSpec((rb, LANES), lambda i: (i, 0))],
        out_shape=[jax.ShapeDtypeStruct((m, DSA_KEY_DIM), BF16), jax.ShapeDtypeStruct((m, LANES), BF16)],
        compiler_params=_cparams("parallel"),
        name="dsa_key_prep",
    )(hr, hr, *tabs)


def _ordered_to_f32(o):
    return lax.bitcast_convert_type(jnp.where(o >= 0, o, o ^ jnp.int32(0x7FFFFFFF)), F32)


def _dsa_kernel(q_ref, qi_ref, km_ref, cq_ref, sq_ref, ci_ref, si_ref, kl_ref, ki_ref, wuv_ref, o_ref,
                sc_ref, qs_ref, qis_ref, wb_ref, m_ref, l_ref, acc_ref, *, tq, bk, topk, n_grp):
    i = pl.program_id(1)
    nh = N_HEADS_DSA
    lane = lax.broadcasted_iota(jnp.int32, (1, LANES), 1)
    scale = DSA_KEY_DIM ** -0.5

    cq, sq = cq_ref[...], sq_ref[...]
    for h in range(nh):
        c0 = h * DSA_KEY_DIM
        xr = _rope_q(q_ref[:, c0:c0 + LANES].astype(F32), cq, sq, lane) * scale
        qs_ref[h * tq:(h + 1) * tq, 0:LANES] = xr.astype(BF16)
        qs_ref[h * tq:(h + 1) * tq, LANES:2 * LANES] = (
            q_ref[:, c0 + LANES:c0 + 2 * LANES].astype(F32) * scale).astype(BF16)
    ci, si = ci_ref[...], si_ref[...]
    km = km_ref[...]
    for p in range(IDX_HEADS // 2):
        xr = _rope_idx(qi_ref[:, p * LANES:(p + 1) * LANES], ci, si, lane)
        for hh in range(2):
            h = 2 * p + hh
            qis_ref[h * tq:(h + 1) * tq, :] = jnp.where((lane >> 6) == hh, xr, 0.0).astype(BF16)
            wb_ref[h * tq:(h + 1) * tq, :] = jnp.broadcast_to(km[:, IDX_DIM + h:IDX_DIM + h + 1], (tq, LANES))

    nkb = lax.div((i + 1) * tq + bk - 1, bk)
    t_abs = i * tq + lax.broadcasted_iota(jnp.int32, (tq, 1), 0)
    vis_lim = ((t_abs >> 6) + 1) << 6
    col0 = lax.broadcasted_iota(jnp.int32, (tq, bk), 1)

    wb2 = jnp.concatenate([wb_ref[...]] * (bk // LANES), axis=1)

    def score_body(kb, _):
        start = pl.multiple_of(kb * bk, bk)
        kd = ki_ref[pl.ds(start, bk), :]
        lg = jnp.maximum(_dot_nt(qis_ref[...], kd), 0.0) * wb2
        s = lg[0:tq]
        for h in range(1, nh):
            s = s + lg[h * tq:(h + 1) * tq]
        s = s * IDX_SCALE
        s = jnp.where(s == 0.0, 0.0, s)
        s = jnp.where(col0 + start < vis_lim, s, -jnp.inf)
        sc_ref[:, pl.ds(start, bk)] = s
        return 0

    lax.fori_loop(0, nkb, score_body, 0)

    def count(pred):
        def body(kb, c):
            start = pl.multiple_of(kb * bk, bk)
            return c + jnp.where(pred(sc_ref[:, pl.ds(start, bk)], col0 + start), 1.0, 0.0)
        c = lax.fori_loop(0, nkb, body, jnp.zeros((tq, bk), F32))
        return jnp.sum(c, axis=1, keepdims=True)

    def bit_body(step, lo):
        cand = lo + lax.shift_left(jnp.int32(1), 31 - step)
        cf = _ordered_to_f32(cand)
        cnt = count(lambda s, c: s >= cf)
        return jnp.where(cnt >= topk, cand, lo)

    lo = lax.fori_loop(0, 32, bit_body, jnp.full((tq, 1), INT_MIN, jnp.int32))
    few = vis_lim <= topk
    thr = jnp.where(few, -jnp.inf, _ordered_to_f32(lo))
    cnt_gt = count(lambda s, c: s > thr)
    cnt_ge = count(lambda s, c: s >= thr)
    need = topk - cnt_gt
    tie = jnp.logical_and(cnt_ge > topk, jnp.logical_not(few))

    def tie_search(_):
        def jb(step, x):
            cand = x + lax.shift_left(jnp.int32(1), 30 - step)
            cnt = count(lambda s, c: jnp.logical_and(s == thr, c < cand))
            return jnp.where(cnt < need, cand, x)
        return lax.fori_loop(0, 31, jb, jnp.zeros((tq, 1), jnp.int32))

    any_tie = jnp.max(jnp.where(tie, 1.0, 0.0)) > 0.0
    jlim = lax.cond(any_tie, tie_search, lambda _: jnp.zeros((tq, 1), jnp.int32), 0)
    jlim = jnp.where(tie, jlim, jnp.int32(2 ** 30))

    def bias_body(kb, _):
        start = pl.multiple_of(kb * bk, bk)
        s = sc_ref[:, pl.ds(start, bk)]
        c = col0 + start
        keep = jnp.logical_or(s > thr, jnp.logical_and(s == thr, c <= jlim))
        keep = jnp.logical_and(keep, c < vis_lim)
        sc_ref[:, pl.ds(start, bk)] = jnp.where(keep, 0.0, NEG)
        return 0

    lax.fori_loop(0, nkb, bias_body, 0)

    m_ref[...] = jnp.full(m_ref.shape, NEG, F32)
    l_ref[...] = jnp.zeros(l_ref.shape, F32)
    acc_ref[...] = jnp.zeros(acc_ref.shape, F32)
    gr = nh * tq // n_grp
    rep = bk // LANES

    def attn_body(kb, _):
        start = pl.multiple_of(kb * bk, bk)
        kl = kl_ref[pl.ds(start, bk), :]
        bias = sc_ref[:, pl.ds(start, bk)]
        bias_g = jnp.concatenate([bias] * (gr // tq), axis=0)
        for g in range(n_grp):
            rows = slice(g * gr, (g + 1) * gr)
            lg = _dot_nt(qs_ref[rows, :], kl) + bias_g
            m_old = m_ref[rows, :]
            m_new = jnp.maximum(m_old, jnp.max(lg, axis=1, keepdims=True))
            p = jnp.exp(lg - jnp.concatenate([m_new] * rep, axis=1))
            a = jnp.exp(m_old - m_new)
            l_ref[rows, :] = a * l_ref[rows, :] + jnp.sum(p, axis=1, keepdims=True)
            acc_ref[rows, :] = jnp.concatenate([a, a], axis=1) * acc_ref[rows, :] + _dot(p.astype(BF16), kl)
            m_ref[rows, :] = m_new
        return 0

    lax.fori_loop(0, nkb, attn_body, 0)

    for p in range(nh // 2):
        parts = []
        for hh in range(2):
            rows = slice((2 * p + hh) * tq, (2 * p + hh + 1) * tq)
            inv = 1.0 / l_ref[rows, :]
            parts.append((acc_ref[rows, :] * jnp.concatenate([inv, inv], axis=1)).astype(BF16))
        ctx = jnp.concatenate(parts, axis=1)
        o_ref[:, p * LANES:(p + 1) * LANES] = _dot(ctx, wuv_ref[p]).astype(o_ref.dtype)


def _dsa(hq, hr, klr, kid, tabs, wuv2, batch, seq, tq, bk):
    nq = seq // tq
    nh = N_HEADS_DSA
    km_blk = (IDX_HEADS * IDX_DIM + DSA_KEY_DIM) // LANES
    topk = min(IDX_TOPK, seq // 4)
    tspec = pl.BlockSpec((tq, LANES), lambda b, i: (b * nq + i, 0))
    return pl.pallas_call(
        functools.partial(_dsa_kernel, tq=tq, bk=bk, topk=topk, n_grp=4),
        grid=(batch, nq),
        in_specs=[pl.BlockSpec((tq, nh * DSA_KEY_DIM), lambda b, i: (b * nq + i, 0)),
                  pl.BlockSpec((tq, IDX_HEADS * IDX_DIM), lambda b, i: (b * nq + i, 0)),
                  pl.BlockSpec((tq, LANES), lambda b, i: (b * nq + i, km_blk)),
                  tspec, tspec, tspec, tspec,
                  pl.BlockSpec((seq, DSA_KEY_DIM), lambda b, i: (b, 0)),
                  pl.BlockSpec((seq, LANES), lambda b, i: (b, 0)),
                  pl.BlockSpec(wuv2.shape, lambda b, i: (0, 0, 0))],
        out_specs=pl.BlockSpec((tq, nh * DSA_V_HEAD), lambda b, i: (b * nq + i, 0)),
        out_shape=jax.ShapeDtypeStruct((batch * seq, nh * DSA_V_HEAD), BF16),
        scratch_shapes=[pltpu.VMEM((tq, seq), F32),
                        pltpu.VMEM((nh * tq, DSA_KEY_DIM), BF16),
                        pltpu.VMEM((nh * tq, LANES), BF16),
                        pltpu.VMEM((nh * tq, LANES), F32),
                        pltpu.VMEM((nh * tq, LANES), F32),
                        pltpu.VMEM((nh * tq, LANES), F32),
                        pltpu.VMEM((nh * tq, DSA_KEY_DIM), F32)],
        compiler_params=_cparams("parallel", "arbitrary"),
        name="dsa_attention",
    )(hq, hr, hr, *tabs, klr, kid, wuv2)


def _rope_tables(positions):
    pos = positions.astype(F32).reshape(-1, 1)
    n = pos.shape[0]

    def cs(half):
        freqs = ROPE_THETA ** (-jnp.arange(half, dtype=F32) / half)
        ang = pos * freqs
        return jnp.cos(ang), jnp.sin(ang)

    cq, sq = cs(DSA_ROPE // 2)
    ones, zeros = jnp.ones((n, LANES - DSA_ROPE), F32), jnp.zeros((n, LANES - DSA_ROPE), F32)
    cos_q = jnp.concatenate([cq, cq, ones], axis=1)
    sin_q = jnp.concatenate([-sq, sq, zeros], axis=1)
    ci, si = cs(IDX_DIM // 8)
    pad = IDX_DIM - IDX_DIM // 4
    ci64 = jnp.concatenate([ci, ci, jnp.ones((n, pad), F32)], axis=1)
    si64 = jnp.concatenate([-si, si, jnp.zeros((n, pad), F32)], axis=1)
    return cos_q, sin_q, jnp.concatenate([ci64, ci64], axis=1), jnp.concatenate([si64, si64], axis=1)


def _pick(n, *cands):
    for c in cands:
        if n % c == 0:
            return c
    return n


def kernel(x, positions, l0_w_in, l0_b_f, l0_w_o, l0_ln1_g, l0_ln1_b, l0_w_up, l0_conv_w, l0_conv_b, l0_w_down, l0_ln2_g, l0_ln2_b, l1_w_in, l1_w_uv, l1_w_o, l1_ln1_g, l1_ln1_b, l1_w_up, l1_conv_w, l1_conv_b, l1_w_down, l1_ln2_g, l1_ln2_b):
    batch, seq, d = x.shape
    m = batch * seq
    xf = x.reshape(m, d).astype(F32)
    xb = xf.astype(BF16)
    bm = _pick(m, 1024, 512, 256)
    tq_a = _pick(seq, 256, 128)

    n_qkv = 3 * (N_HEADS_SB + N_HEADS_FOX) * HEAD_DIM
    h0 = _matmul(xb, l0_w_in[:, :n_qkv].astype(BF16), BF16, bm, 512, "l0_in_proj")
    w_f = jnp.pad(l0_w_in[:, n_qkv:], ((0, 0), (0, LANES - N_HEADS_FOX))).astype(BF16)
    fl = _matmul(xb, w_f, F32, bm, LANES, "l0_forget_proj")
    cum = _forget_cumsum(fl, l0_b_f, batch, seq)
    oa = _sb_attention(h0, batch, seq, tq_a)
    ob = _fox_attention(h0, cum, batch, seq, tq_a)
    wo = l0_w_o.astype(BF16)
    na = N_HEADS_SB * HEAD_DIM
    x1f, x1b = _matmul_ln([(oa, wo[:na]), (ob, wo[na:])], xf, l0_ln1_g, l0_ln1_b, 512, "l0_out_proj_ln")
    u0 = _matmul(x1b, l0_w_up.astype(BF16), BF16, bm, 512, "l0_ffn_up")
    x2f, x2b = _ffn_down(u0, l0_conv_w, l0_conv_b, l0_w_down.astype(BF16), x1f, l0_ln2_g, l0_ln2_b,
                         seq, 256, "l0_ffn_down_ln")

    c0 = N_HEADS_DSA * DSA_KEY_DIM
    c2 = c0 + DSA_KEY_DIM
    c3 = c2 + IDX_HEADS * IDX_DIM
    n_in = l1_w_in.shape[1]
    hq = _matmul(x2b, l1_w_in[:, :c0].astype(BF16), BF16, bm, 512, "l1_q_proj")
    w_rest = jnp.concatenate([l1_w_in[:, c2:c3], l1_w_in[:, c0:c2], l1_w_in[:, c3:],
                              jnp.zeros((d, LANES - (n_in - c3)), F32)], axis=1).astype(BF16)
    hr = _matmul(x2b, w_rest, F32, 512, w_rest.shape[1], "l1_kidx_proj")
    tabs = _rope_tables(positions)
    klr, kid = _dsa_prep(hr, tabs, 512)
    wuv = l1_w_uv.astype(BF16)
    zpad = jnp.zeros((DSA_ROPE, DSA_V_HEAD), BF16)
    zblk = jnp.zeros((DSA_KEY_DIM, DSA_V_HEAD), BF16)
    wuv2 = jnp.stack([
        jnp.concatenate([jnp.concatenate([zpad, wuv[2 * p], zblk], axis=0),
                         jnp.concatenate([zblk, zpad, wuv[2 * p + 1]], axis=0)], axis=1)
        for p in range(N_HEADS_DSA // 2)])
    o1 = _dsa(hq, hr, klr, kid, tabs, wuv2, batch, seq, 128, 256)
    x3f, x3b = _matmul_ln([(o1, l1_w_o.astype(BF16))], x2f, l1_ln1_g, l1_ln1_b, 512, "l1_out_proj_ln")
    u1 = _matmul(x3b, l1_w_up.astype(BF16), BF16, bm, 512, "l1_ffn_up")
    x4f, _ = _ffn_down(u1, l1_conv_w, l1_conv_b, l1_w_down.astype(BF16), x3f, l1_ln2_g, l1_ln2_b,
                       seq, 256, "l1_ffn_down_ln")
    return x4f.reshape(batch, seq, d)
```

```python
import functools

import jax
import jax.numpy as jnp
from jax import lax
from jax.experimental import pallas as pl
from jax.experimental.pallas import tpu as pltpu

F32 = jnp.float32
BF16 = jnp.bfloat16

LANES = 128
VMEM_LIMIT = 56 * 1024 * 1024

HEAD_DIM = 64
N_HEADS_SB = 8
N_HEADS_FOX = 8
N_HEADS_DSA = 16
DSA_KEY_DIM = 256
DSA_ROPE = 64
DSA_V_HEAD = 64
IDX_HEADS = 16
IDX_DIM = 64
IDX_TOPK = 256
IDX_SCALE = (IDX_HEADS * IDX_DIM) ** -0.5
CHUNK = 64
ROPE_THETA = 500000.0
D_FF = 2816
LN_EPS = 1e-5
DEPTH = 2
ALPHA = (2 * DEPTH) ** 0.25
NEG = -1e30
INT_MIN = -(2 ** 31)
LOG2E = 1.4426950408889634


def _cparams(*sem):
    return pltpu.CompilerParams(dimension_semantics=sem, vmem_limit_bytes=VMEM_LIMIT)


def _dot(a, b):
    return jnp.dot(a, b, preferred_element_type=F32)


def _dot_nt(a, b):
    return lax.dot_general(a, b, (((1,), (1,)), ((), ())), preferred_element_type=F32)


def _split3(x):
    x1 = x.astype(BF16)
    r1 = x - x1.astype(F32)
    x2 = r1.astype(BF16)
    x3 = (r1 - x2.astype(F32)).astype(BF16)
    return x1, x2, x3


def _mm_kernel(x_ref, w_ref, o_ref):
    o_ref[...] = _dot(x_ref[...], w_ref[...]).astype(o_ref.dtype)


def _matmul(x, w, out_dtype, bm, bn, name):
    m, k = x.shape
    n = w.shape[1]
    return pl.pallas_call(
        _mm_kernel,
        grid=(m // bm, n // bn),
        in_specs=[pl.BlockSpec((bm, k), lambda i, j: (i, 0)),
                  pl.BlockSpec((k, bn), lambda i, j: (0, j))],
        out_specs=pl.BlockSpec((bm, bn), lambda i, j: (i, j)),
        out_shape=jax.ShapeDtypeStruct((m, n), out_dtype),
        compiler_params=_cparams("parallel", "parallel"),
        name=name,
    )(x, w)


def _layer_norm_rows(y, g, b):
    mu = jnp.mean(y, axis=-1, keepdims=True)
    yc = y - mu
    var = jnp.mean(yc * yc, axis=-1, keepdims=True)
    return yc * lax.rsqrt(var + LN_EPS) * g + b


def _mm_ln_kernel(*refs, n_pairs):
    xs = refs[:n_pairs]
    ws = refs[n_pairs:2 * n_pairs]
    res_ref, g_ref, b_ref, of_ref, ob_ref = refs[2 * n_pairs:]
    acc = _dot(xs[0][...], ws[0][...])
    for p in range(1, n_pairs):
        acc = acc + _dot(xs[p][...], ws[p][...])
    out = _layer_norm_rows(ALPHA * res_ref[...] + acc, g_ref[...], b_ref[...])
    of_ref[...] = out
    ob_ref[...] = out.astype(BF16)


def _matmul_ln(pairs, res, g, b, bm, name):
    m, d = res.shape
    n_pairs = len(pairs)
    in_specs = ([pl.BlockSpec((bm, x.shape[1]), lambda i: (i, 0)) for x, _ in pairs]
                + [pl.BlockSpec(w.shape, lambda i: (0, 0)) for _, w in pairs]
                + [pl.BlockSpec((bm, d), lambda i: (i, 0)),
                   pl.BlockSpec((1, d), lambda i: (0, 0)),
                   pl.BlockSpec((1, d), lambda i: (0, 0))])
    return pl.pallas_call(
        functools.partial(_mm_ln_kernel, n_pairs=n_pairs),
        grid=(m // bm,),
        in_specs=in_specs,
        out_specs=[pl.BlockSpec((bm, d), lambda i: (i, 0)),
                   pl.BlockSpec((bm, d), lambda i: (i, 0))],
        out_shape=[jax.ShapeDtypeStruct((m, d), F32), jax.ShapeDtypeStruct((m, d), BF16)],
        compiler_params=_cparams("parallel"),
        name=name,
    )(*[x for x, _ in pairs], *[w for _, w in pairs], res, g.reshape(1, d), b.reshape(1, d))


def _shift_rows(g, halo, k, row):
    out = pltpu.roll(g, k, 0)
    for r in range(k):
        out = jnp.where(row == r, halo[8 - k + r:8 - k + r + 1, :], out)
    return out


def _ffn_down_kernel(g_ref, gh_ref, u_ref, cw_ref, cb_ref, wd_ref, res_ref, lg_ref, lb_ref,
                     of_ref, ob_ref, *, bm, seq):
    i = pl.program_id(0)
    g = g_ref[...].astype(F32)
    starts_seq = lax.rem(i * bm, seq) == 0
    halo = jnp.where(starts_seq, 0.0, gh_ref[...].astype(F32))
    row = lax.broadcasted_iota(jnp.int32, g.shape, 0)
    g1 = _shift_rows(g, halo, 1, row)
    g2 = _shift_rows(g, halo, 2, row)
    cw = cw_ref[...]
    conv = cb_ref[...] + cw[0:1, :] * g2
    conv = conv + cw[1:2, :] * g1
    conv = conv + cw[2:3, :] * g
    cdf = 0.5 * (1.0 + jnp.tanh(0.7978845608028654 * (conv + 0.044715 * (conv * conv * conv))))
    act = (conv * cdf) * u_ref[...].astype(F32)
    acc = _dot(act.astype(BF16), wd_ref[...])
    out = _layer_norm_rows(ALPHA * res_ref[...] + acc, lg_ref[...], lb_ref[...])
    of_ref[...] = out
    ob_ref[...] = out.astype(BF16)


def _ffn_down(u, conv_w, conv_b, w_down, res, g, b, seq, bm, name):
    m, d = res.shape
    f = w_down.shape[0]
    hb = bm // 8
    return pl.pallas_call(
        functools.partial(_ffn_down_kernel, bm=bm, seq=seq),
        grid=(m // bm,),
        in_specs=[pl.BlockSpec((bm, f), lambda i: (i, 0)),
                  pl.BlockSpec((8, f), lambda i: (jnp.maximum(i * hb - 1, 0), 0)),
                  pl.BlockSpec((bm, f), lambda i: (i, 1)),
                  pl.BlockSpec((3, f), lambda i: (0, 0)),
                  pl.BlockSpec((1, f), lambda i: (0, 0)),
                  pl.BlockSpec((f, d), lambda i: (0, 0)),
                  pl.BlockSpec((bm, d), lambda i: (i, 0)),
                  pl.BlockSpec((1, d), lambda i: (0, 0)),
                  pl.BlockSpec((1, d), lambda i: (0, 0))],
        out_specs=[pl.BlockSpec((bm, d), lambda i: (i, 0)),
                   pl.BlockSpec((bm, d), lambda i: (i, 0))],
        out_shape=[jax.ShapeDtypeStruct((m, d), F32), jax.ShapeDtypeStruct((m, d), BF16)],
        compiler_params=_cparams("parallel"),
        name=name,
    )(u, u, u, conv_w, conv_b.reshape(1, f), w_down, res, g.reshape(1, d), b.reshape(1, d))


def _cum_kernel(f_ref, bias_ref, o_ref, *, rows_per_seq):
    x = f_ref[...] + bias_ref[...]
    ls = jnp.minimum(x, 0.0) - jnp.log(1.0 + jnp.exp(-jnp.abs(x)))
    r = x.shape[0]
    incl = (lax.broadcasted_iota(jnp.int32, (LANES, LANES), 0)
            <= lax.broadcasted_iota(jnp.int32, (LANES, LANES), 1)).astype(BF16)
    a1, a2, a3 = _split3(ls)
    cs = _dot(a1, incl) + _dot(a2, incl) + _dot(a3, incl)
    tot = jnp.broadcast_to(cs[:, LANES - 1:LANES], (r, LANES))
    ri = lax.broadcasted_iota(jnp.int32, (r, r), 0)
    ci = lax.broadcasted_iota(jnp.int32, (r, r), 1)
    shift = rows_per_seq.bit_length() - 1
    same_seq = (ri >> shift) == (ci >> shift)
    before = jnp.where(same_seq & (ci < ri), 1.0, 0.0).astype(BF16)
    t1, t2, t3 = _split3(tot)
    off = _dot(before, t1) + _dot(before, t2) + _dot(before, t3)
    o_ref[...] = cs + off


def _forget_cumsum(fl, b_f, batch, seq):
    nh = N_HEADS_FOX
    f = fl[:, :nh].reshape(batch, seq, nh).transpose(0, 2, 1).reshape(batch * nh * (seq // LANES), LANES)
    bias = jnp.broadcast_to(jnp.tile(b_f.astype(F32), batch)[:, None, None],
                            (batch * nh, seq // LANES, LANES)).reshape(f.shape)
    cum = pl.pallas_call(
        functools.partial(_cum_kernel, rows_per_seq=seq // LANES),
        out_shape=jax.ShapeDtypeStruct(f.shape, F32),
        compiler_params=pltpu.CompilerParams(vmem_limit_bytes=VMEM_LIMIT),
        name="forget_cumsum",
    )(f, bias)
    return cum.reshape(batch * nh, seq)


def _stack_heads(q, lane, scale):
    qf = q.astype(F32) * scale
    return jnp.concatenate([jnp.where(lane < HEAD_DIM, qf, 0.0), jnp.where(lane >= HEAD_DIM, qf, 0.0)],
                           axis=0).astype(BF16)


def _sb_kernel(q_ref, k_ref, v_ref, o_ref, acc_ref, c_ref, z_ref, *, tq, bk):
    i = pl.program_id(2)
    lane = lax.broadcasted_iota(jnp.int32, (1, LANES), 1)
    qs = _stack_heads(q_ref[...], lane, HEAD_DIM ** -0.5 * LOG2E)
    rj = lax.broadcasted_iota(jnp.int32, (2 * bk, bk), 0)
    cs_ = lax.broadcasted_iota(jnp.int32, (2 * bk, bk), 1)
    neg_suffix = jnp.where((rj & (bk - 1)) >= cs_, -1.0, 0.0).astype(BF16)
    row = lax.broadcasted_iota(jnp.int32, (tq, bk), 0) + i * tq
    col = lax.broadcasted_iota(jnp.int32, (tq, bk), 1)
    acc_ref[...] = jnp.zeros(acc_ref.shape, F32)
    c_ref[...] = jnp.zeros(c_ref.shape, F32)

    def scores(kb):
        start = pl.multiple_of(kb * bk, bk)
        return _dot_nt(qs, k_ref[pl.ds(start, bk), :])

    def step(kb, masked, nxt):
        z = z_ref[...]
        if nxt is not None:
            z_ref[...] = scores(nxt)
        start = pl.multiple_of(kb * bk, bk)
        v = v_ref[pl.ds(start, bk), :]
        sp = jnp.maximum(z, 0.0) + jnp.log(1.0 + jnp.exp2(-jnp.abs(z))) * LOG2E
        if masked:
            keep = (col + start) < row
            keep = jnp.concatenate([keep, keep], axis=0)
            sp = jnp.where(keep, sp, 0.0)
        hi = sp.astype(BF16)
        lo = (sp - hi.astype(F32)).astype(BF16)
        cs = _dot(jnp.concatenate([hi, lo], axis=1), neg_suffix)
        c = c_ref[...]
        w = jnp.exp2(z + cs + jnp.concatenate([c] * (bk // LANES), axis=1))
        if masked:
            w = jnp.where(keep, w, 0.0)
        acc_ref[...] += _dot(w.astype(BF16), v)
        c_ref[...] = c + jnp.broadcast_to(cs[:, 0:1], c.shape)

    per = tq // bk
    top = i * per + per - 1
    z_ref[...] = scores(top)
    for r in range(per - 1):
        step(top - r, True, top - r - 1)

    @pl.when(i == 0)
    def _():
        step(0, True, None)

    @pl.when(i > 0)
    def _():
        step(i * per, True, i * per - 1)

        def body(u, _):
            kb = i * per - 1 - 2 * u
            step(kb, False, kb - 1)
            step(kb - 1, False, kb - 2)
            return 0

        lax.fori_loop(0, i * (per // 2) - 1, body, 0)
        step(1, False, 0)
        step(0, False, None)

    o_ref[...] = jnp.where(lane < HEAD_DIM, acc_ref[0:tq, :], acc_ref[tq:2 * tq, :]).astype(o_ref.dtype)


def _sb_attention(h0, batch, seq, tq, bk):
    assert tq == 2 * bk and seq % tq == 0, (seq, tq, bk)
    npair = N_HEADS_SB // 2
    nq = seq // tq
    q_off, k_off, v_off = 0, npair, 2 * npair
    return pl.pallas_call(
        functools.partial(_sb_kernel, tq=tq, bk=bk),
        grid=(batch, npair, nq),
        in_specs=[pl.BlockSpec((tq, LANES), lambda b, p, i: (b * nq + i, q_off + p)),
                  pl.BlockSpec((seq, LANES), lambda b, p, i: (b, k_off + p)),
                  pl.BlockSpec((seq, LANES), lambda b, p, i: (b, v_off + p))],
        out_specs=pl.BlockSpec((tq, LANES), lambda b, p, i: (b * nq + i, p)),
        out_shape=jax.ShapeDtypeStruct((batch * seq, N_HEADS_SB * HEAD_DIM), BF16),
        scratch_shapes=[pltpu.VMEM((2 * tq, LANES), F32),
                        pltpu.VMEM((2 * tq, LANES), F32),
                        pltpu.VMEM((2 * tq, bk), F32)],
        compiler_params=_cparams("parallel", "parallel", "arbitrary"),
        name="sb_attention",
    )(h0, h0, h0)


def _fox_kernel(q_ref, k_ref, v_ref, cq_ref, ck_ref, o_ref, m_ref, acc_ref, z_ref, *, tq):
    i = pl.program_id(2)
    lane = lax.broadcasted_iota(jnp.int32, (1, LANES), 1)
    qs = _stack_heads(q_ref[...], lane, HEAD_DIM ** -0.5 * LOG2E)
    cq = jnp.concatenate([cq_ref[0], cq_ref[1]], axis=0) * LOG2E
    row = lax.broadcasted_iota(jnp.int32, (tq, tq), 0)
    col = lax.broadcasted_iota(jnp.int32, (tq, tq), 1)
    causal = jnp.concatenate([col <= row, col <= row], axis=0)
    ones = jnp.ones((tq, LANES), BF16)
    m_ref[...] = jnp.full(m_ref.shape, NEG, F32)
    acc_ref[...] = jnp.zeros(acc_ref.shape, F32)

    def scores(kb):
        start = pl.multiple_of(kb * tq, tq)
        k = k_ref[pl.ds(start, tq), :]
        ck = jnp.concatenate([jnp.broadcast_to(ck_ref[0, :, pl.ds(start, tq)] * LOG2E, (tq, tq)),
                              jnp.broadcast_to(ck_ref[1, :, pl.ds(start, tq)] * LOG2E, (tq, tq))], axis=0)
        return (_dot_nt(qs, k) + cq) - ck

    def step(kb, masked, nxt):
        lg = z_ref[...]
        if nxt is not None:
            z_ref[...] = scores(nxt)
        if masked:
            lg = jnp.where(causal, lg, NEG)
        start = pl.multiple_of(kb * tq, tq)
        v = v_ref[pl.ds(start, tq), :]
        m_old = m_ref[...]
        m_new = jnp.maximum(m_old, jnp.max(lg, axis=1, keepdims=True))
        p = jnp.exp2(lg - jnp.concatenate([m_new] * (tq // LANES), axis=1))
        a = jnp.exp2(m_old - m_new)
        acc_ref[...] = (jnp.concatenate([a, a], axis=1) * acc_ref[...]
                        + _dot(p.astype(BF16), jnp.concatenate([v, ones], axis=1)))
        m_ref[...] = m_new

    z_ref[...] = scores(i)

    @pl.when(i == 0)
    def _():
        step(0, True, None)

    @pl.when(i > 0)
    def _():
        step(i, True, i - 1)

        def body(t, _):
            step(i - t, False, i - t - 1)
            return 0

        lax.fori_loop(1, i, body, 0)
        step(0, False, None)

    out = acc_ref[:, 0:LANES] / acc_ref[:, LANES:2 * LANES]
    o_ref[...] = jnp.where(lane < HEAD_DIM, out[0:tq], out[tq:2 * tq]).astype(o_ref.dtype)


def _fox_attention(h0, cum, batch, seq, tq):
    npair = N_HEADS_FOX // 2
    nq = seq // tq
    base = 3 * (N_HEADS_SB // 2)
    q_off, k_off, v_off = base, base + npair, base + 2 * npair
    cum_q = cum.reshape(batch * N_HEADS_FOX, seq, 1)
    cum_k = cum.reshape(batch * N_HEADS_FOX, 1, seq)
    return pl.pallas_call(
        functools.partial(_fox_kernel, tq=tq),
        grid=(batch, npair, nq),
        in_specs=[pl.BlockSpec((tq, LANES), lambda b, p, i: (b * nq + i, q_off + p)),
                  pl.BlockSpec((seq, LANES), lambda b, p, i: (b, k_off + p)),
                  pl.BlockSpec((seq, LANES), lambda b, p, i: (b, v_off + p)),
                  pl.BlockSpec((2, tq, 1), lambda b, p, i: (b * npair + p, i, 0)),
                  pl.BlockSpec((2, 1, seq), lambda b, p, i: (b * npair + p, 0, 0))],
        out_specs=pl.BlockSpec((tq, LANES), lambda b, p, i: (b * nq + i, p)),
        out_shape=jax.ShapeDtypeStruct((batch * seq, N_HEADS_FOX * HEAD_DIM), BF16),
        scratch_shapes=[pltpu.VMEM((2 * tq, LANES), F32),
                        pltpu.VMEM((2 * tq, 2 * LANES), F32),
                        pltpu.VMEM((2 * tq, tq), F32)],
        compiler_params=_cparams("parallel", "parallel", "arbitrary"),
        name="fox_attention",
    )(h0, h0, h0, cum_q, cum_k)


def _rope_q(x, cos_t, sin_t, lane):
    xs = jnp.where(lane < DSA_ROPE // 2, pltpu.roll(x, LANES - DSA_ROPE // 2, 1), pltpu.roll(x, DSA_ROPE // 2, 1))
    return x * cos_t + xs * sin_t


def _rope_idx(x, cos_t, sin_t, lane):
    half = IDX_DIM // 8
    xs = jnp.where((lane & (IDX_DIM - 1)) < half, pltpu.roll(x, LANES - half, 1), pltpu.roll(x, half, 1))
    return x * cos_t + xs * sin_t


def _dsa_prep_kernel(kl_ref, km_ref, cq_ref, sq_ref, ci_ref, si_ref, klo_ref, kio_ref):
    lane = lax.broadcasted_iota(jnp.int32, (1, LANES), 1)
    klo_ref[:, 0:LANES] = _rope_q(kl_ref[:, 0:LANES], cq_ref[...], sq_ref[...], lane).astype(BF16)
    klo_ref[:, LANES:2 * LANES] = kl_ref[:, LANES:2 * LANES].astype(BF16)
    yr = _rope_idx(km_ref[...], ci_ref[...], si_ref[...], lane)
    kio_ref[...] = jnp.where(lane < IDX_DIM, yr, pltpu.roll(yr, IDX_DIM, 1)).astype(BF16)


def _dsa_prep(hr, tabs, rb):
    m = hr.shape[0]
    kl_blk = IDX_HEADS * IDX_DIM // (2 * LANES)
    km_blk = (IDX_HEADS * IDX_DIM + DSA_KEY_DIM) // LANES
    tspec = pl.BlockSpec((rb, LANES), lambda i: (i, 0))
    return pl.pallas_call(
        _dsa_prep_kernel,
        grid=(m // rb,),
        in_specs=[pl.BlockSpec((rb, 2 * LANES), lambda i: (i, kl_blk)),
                  pl.BlockSpec((rb, LANES), lambda i: (i, km_blk)),
                  tspec, tspec, tspec, tspec],
        out_specs=[pl.BlockSpec((rb, 2 * LANES), lambda i: (i, 0)),
                   pl.BlockSpec((rb, LANES), lambda i: (i, 0))],
        out_shape=[jax.ShapeDtypeStruct((m, DSA_KEY_DIM), BF16), jax.ShapeDtypeStruct((m, LANES), BF16)],
        compiler_params=_cparams("parallel"),
        name="dsa_key_prep",
    )(hr, hr, *tabs)


def _ordered_to_f32(o):
    return lax.bitcast_convert_type(jnp.where(o >= 0, o, o ^ jnp.int32(0x7FFFFFFF)), F32)


def _dsa_kernel(q_ref, qi_ref, km_ref, cq_ref, sq_ref, ci_ref, si_ref, kl_ref, ki_ref, wuv_ref, o_ref,
                sc_ref, qs_ref, qis_ref, wb_ref, m_ref, acc_ref, z_ref, *, tq, bk, topk, n_grp):
    i = pl.program_id(1)
    nh = N_HEADS_DSA
    lane = lax.broadcasted_iota(jnp.int32, (1, LANES), 1)
    scale = DSA_KEY_DIM ** -0.5 * LOG2E
    gr = nh * tq // n_grp
    rep = bk // LANES

    cq, sq = cq_ref[...], sq_ref[...]
    for h in range(nh):
        c0 = h * DSA_KEY_DIM
        xr = _rope_q(q_ref[:, c0:c0 + LANES].astype(F32), cq, sq, lane) * scale
        qs_ref[h * tq:(h + 1) * tq, 0:LANES] = xr.astype(BF16)
        qs_ref[h * tq:(h + 1) * tq, LANES:2 * LANES] = (
            q_ref[:, c0 + LANES:c0 + 2 * LANES].astype(F32) * scale).astype(BF16)
    ci, si = ci_ref[...], si_ref[...]
    km = km_ref[...]
    for p in range(IDX_HEADS // 2):
        xr = _rope_idx(qi_ref[:, p * LANES:(p + 1) * LANES], ci, si, lane)
        for hh in range(2):
            h = 2 * p + hh
            qis_ref[h * tq:(h + 1) * tq, :] = jnp.where((lane >> 6) == hh, xr, 0.0).astype(BF16)
            wb_ref[h * tq:(h + 1) * tq, :] = jnp.broadcast_to(km[:, IDX_DIM + h:IDX_DIM + h + 1], (tq, LANES))

    nkb = lax.div((i + 1) * tq + bk - 1, bk)
    t_abs = i * tq + lax.broadcasted_iota(jnp.int32, (tq, 1), 0)
    vis_lim = ((t_abs >> 6) + 1) << 6
    col0 = lax.broadcasted_iota(jnp.int32, (tq, bk), 1)

    def score_body(kb, _):
        start = pl.multiple_of(kb * bk, bk)
        kd = ki_ref[pl.ds(start, bk), :]
        s = jnp.zeros((tq, bk), F32)
        for g in range(n_grp):
            rows = slice(g * gr, (g + 1) * gr)
            wb = jnp.concatenate([wb_ref[rows, :]] * rep, axis=1)
            lg = jnp.maximum(_dot_nt(qis_ref[rows, :], kd), 0.0) * wb
            for h in range(gr // tq):
                s = s + lg[h * tq:(h + 1) * tq]
        s = s * IDX_SCALE
        s = jnp.where(s == 0.0, 0.0, s)
        s = jnp.where(col0 + start < vis_lim, s, -jnp.inf)
        sc_ref[:, pl.ds(start, bk)] = s
        return 0

    lax.fori_loop(0, nkb, score_body, 0)

    def count(pred):
        def body(kb, c):
            start = pl.multiple_of(kb * bk, bk)
            hit = jnp.where(pred(sc_ref[:, pl.ds(start, bk)], col0 + start), 1.0, 0.0)
            for j in range(rep):
                c = c + hit[:, j * LANES:(j + 1) * LANES]
            return c
        c = lax.fori_loop(0, nkb, body, jnp.zeros((tq, LANES), F32))
        return jnp.sum(c, axis=1, keepdims=True)

    def bit_body(step, lo):
        cand = lo + lax.shift_left(jnp.int32(1), 31 - step)
        cf = _ordered_to_f32(cand)
        cnt = count(lambda s, c: s >= cf)
        return jnp.where(cnt >= topk, cand, lo)

    lo = lax.fori_loop(0, 32, bit_body, jnp.full((tq, 1), INT_MIN, jnp.int32))
    few = vis_lim <= topk
    thr = jnp.where(few, -jnp.inf, _ordered_to_f32(lo))
    cnt_gt = count(lambda s, c: s > thr)
    cnt_ge = count(lambda s, c: s >= thr)
    need = topk - cnt_gt
    tie = jnp.logical_and(cnt_ge > topk, jnp.logical_not(few))

    def tie_search(_):
        def jb(step, x):
            cand = x + lax.shift_left(jnp.int32(1), 30 - step)
            cnt = count(lambda s, c: jnp.logical_and(s == thr, c < cand))
            return jnp.where(cnt < need, cand, x)
        return lax.fori_loop(0, 31, jb, jnp.zeros((tq, 1), jnp.int32))

    any_tie = jnp.max(jnp.where(tie, 1.0, 0.0)) > 0.0
    jlim = lax.cond(any_tie, tie_search, lambda _: jnp.zeros((tq, 1), jnp.int32), 0)
    jlim = jnp.where(tie, jlim, jnp.int32(2 ** 30))

    def bias_body(kb, _):
        start = pl.multiple_of(kb * bk, bk)
        s = sc_ref[:, pl.ds(start, bk)]
        c = col0 + start
        keep = jnp.logical_or(s > thr, jnp.logical_and(s == thr, c <= jlim))
        keep = jnp.logical_and(keep, c < vis_lim)
        sc_ref[:, pl.ds(start, bk)] = jnp.where(keep, 0.0, NEG)
        return 0

    lax.fori_loop(0, nkb, bias_body, 0)

    m_ref[...] = jnp.full(m_ref.shape, NEG, F32)
    acc_ref[...] = jnp.zeros(acc_ref.shape, F32)
    lane2 = lax.broadcasted_iota(jnp.int32, (1, DSA_KEY_DIM), 1)

    def qk(kb):
        start = pl.multiple_of(kb * bk, bk)
        kl = kl_ref[pl.ds(start, bk), :]
        for g in range(n_grp):
            rows = slice(g * gr, (g + 1) * gr)
            z_ref[rows, :] = _dot_nt(qs_ref[rows, :], kl)

    def attn_step(kb, prefetch):
        start = pl.multiple_of(kb * bk, bk)
        kl = kl_ref[pl.ds(start, bk), :]
        vl = jnp.where(lane2 < DSA_ROPE, jnp.ones_like(kl), kl)
        bias = sc_ref[:, pl.ds(start, bk)]
        bias_g = jnp.concatenate([bias] * (gr // tq), axis=0)
        if prefetch:
            nstart = pl.multiple_of((kb + 1) * bk, bk)
            kn = kl_ref[pl.ds(nstart, bk), :]
        for g in range(n_grp):
            rows = slice(g * gr, (g + 1) * gr)
            lg = z_ref[rows, :] + bias_g
            if prefetch:
                z_ref[rows, :] = _dot_nt(qs_ref[rows, :], kn)
            m_old = m_ref[rows, :]
            m_new = jnp.maximum(m_old, jnp.max(lg, axis=1, keepdims=True))
            p = jnp.exp2(lg - jnp.concatenate([m_new] * rep, axis=1))
            a = jnp.exp2(m_old - m_new)
            acc_ref[rows, :] = jnp.concatenate([a, a], axis=1) * acc_ref[rows, :] + _dot(p.astype(BF16), vl)
            m_ref[rows, :] = m_new

    qk(0)

    def attn_body(kb, _):
        attn_step(kb, True)
        return 0

    lax.fori_loop(0, nkb - 1, attn_body, 0)
    attn_step(nkb - 1, False)

    for p in range(nh // 2):
        parts = []
        for hh in range(2):
            rows = slice((2 * p + hh) * tq, (2 * p + hh + 1) * tq)
            acc = acc_ref[rows, :]
            inv = 1.0 / jnp.broadcast_to(acc[:, 0:1], acc.shape)
            parts.append((acc * inv).astype(BF16))
        ctx = jnp.concatenate(parts, axis=1)
        o_ref[:, p * LANES:(p + 1) * LANES] = _dot(ctx, wuv_ref[p]).astype(o_ref.dtype)


def _dsa(hq, hr, klr, kid, tabs, wuv2, batch, seq, tq, bk):
    nq = seq // tq
    nh = N_HEADS_DSA
    km_blk = (IDX_HEADS * IDX_DIM + DSA_KEY_DIM) // LANES
    topk = min(IDX_TOPK, seq // 4)
    tspec = pl.BlockSpec((tq, LANES), lambda b, i: (b * nq + i, 0))
    return pl.pallas_call(
        functools.partial(_dsa_kernel, tq=tq, bk=bk, topk=topk, n_grp=4),
        grid=(batch, nq),
        in_specs=[pl.BlockSpec((tq, nh * DSA_KEY_DIM), lambda b, i: (b * nq + i, 0)),
                  pl.BlockSpec((tq, IDX_HEADS * IDX_DIM), lambda b, i: (b * nq + i, 0)),
                  pl.BlockSpec((tq, LANES), lambda b, i: (b * nq + i, km_blk)),
                  tspec, tspec, tspec, tspec,
                  pl.BlockSpec((seq, DSA_KEY_DIM), lambda b, i: (b, 0)),
                  pl.BlockSpec((seq, LANES), lambda b, i: (b, 0)),
                  pl.BlockSpec(wuv2.shape, lambda b, i: (0, 0, 0))],
        out_specs=pl.BlockSpec((tq, nh * DSA_V_HEAD), lambda b, i: (b * nq + i, 0)),
        out_shape=jax.ShapeDtypeStruct((batch * seq, nh * DSA_V_HEAD), BF16),
        scratch_shapes=[pltpu.VMEM((tq, seq), F32),
                        pltpu.VMEM((nh * tq, DSA_KEY_DIM), BF16),
                        pltpu.VMEM((nh * tq, LANES), BF16),
                        pltpu.VMEM((nh * tq, LANES), F32),
                        pltpu.VMEM((nh * tq, LANES), F32),
                        pltpu.VMEM((nh * tq, DSA_KEY_DIM), F32),
                        pltpu.VMEM((nh * tq, bk), F32)],
        compiler_params=_cparams("parallel", "arbitrary"),
        name="dsa_attention",
    )(hq, hr, hr, *tabs, klr, kid, wuv2)


def _rope_tables(positions):
    pos = positions.astype(F32).reshape(-1, 1)
    n = pos.shape[0]

    def cs(half):
        freqs = ROPE_THETA ** (-jnp.arange(half, dtype=F32) / half)
        ang = pos * freqs
        return jnp.cos(ang), jnp.sin(ang)

    cq, sq = cs(DSA_ROPE // 2)
    ones, zeros = jnp.ones((n, LANES - DSA_ROPE), F32), jnp.zeros((n, LANES - DSA_ROPE), F32)
    cos_q = jnp.concatenate([cq, cq, ones], axis=1)
    sin_q = jnp.concatenate([-sq, sq, zeros], axis=1)
    ci, si = cs(IDX_DIM // 8)
    pad = IDX_DIM - IDX_DIM // 4
    ci64 = jnp.concatenate([ci, ci, jnp.ones((n, pad), F32)], axis=1)
    si64 = jnp.concatenate([-si, si, jnp.zeros((n, pad), F32)], axis=1)
    return cos_q, sin_q, jnp.concatenate([ci64, ci64], axis=1), jnp.concatenate([si64, si64], axis=1)


def _pick(n, *cands):
    for c in cands:
        if n % c == 0:
            return c
    return n


def kernel(x, positions, l0_w_in, l0_b_f, l0_w_o, l0_ln1_g, l0_ln1_b, l0_w_up, l0_conv_w, l0_conv_b, l0_w_down, l0_ln2_g, l0_ln2_b, l1_w_in, l1_w_uv, l1_w_o, l1_ln1_g, l1_ln1_b, l1_w_up, l1_conv_w, l1_conv_b, l1_w_down, l1_ln2_g, l1_ln2_b):
    batch, seq, d = x.shape
    m = batch * seq
    xf = x.reshape(m, d).astype(F32)
    xb = xf.astype(BF16)
    bm = _pick(m, 1024, 512, 256)
    tq_a = _pick(seq, 512, 256, 128)
    bk_sb = _pick(tq_a, 256, 128)

    n_qkv = 3 * (N_HEADS_SB + N_HEADS_FOX) * HEAD_DIM
    h0 = _matmul(xb, l0_w_in[:, :n_qkv].astype(BF16), BF16, bm, 512, "l0_in_proj")
    w_f = jnp.pad(l0_w_in[:, n_qkv:], ((0, 0), (0, LANES - N_HEADS_FOX))).astype(BF16)
    fl = _matmul(xb, w_f, F32, bm, LANES, "l0_forget_proj")
    cum = _forget_cumsum(fl, l0_b_f, batch, seq)
    oa = _sb_attention(h0, batch, seq, tq_a, bk_sb)
    ob = _fox_attention(h0, cum, batch, seq, tq_a)
    wo = l0_w_o.astype(BF16)
    na = N_HEADS_SB * HEAD_DIM
    x1f, x1b = _matmul_ln([(oa, wo[:na]), (ob, wo[na:])], xf, l0_ln1_g, l0_ln1_b, 512, "l0_out_proj_ln")
    u0 = _matmul(x1b, l0_w_up.astype(BF16), BF16, bm, 512, "l0_ffn_up")
    x2f, x2b = _ffn_down(u0, l0_conv_w, l0_conv_b, l0_w_down.astype(BF16), x1f, l0_ln2_g, l0_ln2_b,
                         seq, 256, "l0_ffn_down_ln")

    c0 = N_HEADS_DSA * DSA_KEY_DIM
    c2 = c0 + DSA_KEY_DIM
    c3 = c2 + IDX_HEADS * IDX_DIM
    n_in = l1_w_in.shape[1]
    hq = _matmul(x2b, l1_w_in[:, :c0].astype(BF16), BF16, bm, 512, "l1_q_proj")
    w_rest = jnp.concatenate([l1_w_in[:, c2:c3], l1_w_in[:, c0:c2], l1_w_in[:, c3:],
                              jnp.zeros((d, LANES - (n_in - c3)), F32)], axis=1).astype(BF16)
    hr = _matmul(x2b, w_rest, F32, 512, w_rest.shape[1], "l1_kidx_proj")
    tabs = _rope_tables(positions)
    klr, kid = _dsa_prep(hr, tabs, 512)
    wuv = l1_w_uv.astype(BF16)
    zpad = jnp.zeros((DSA_ROPE, DSA_V_HEAD), BF16)
    zblk = jnp.zeros((DSA_KEY_DIM, DSA_V_HEAD), BF16)
    wuv2 = jnp.stack([
        jnp.concatenate([jnp.concatenate([zpad, wuv[2 * p], zblk], axis=0),
                         jnp.concatenate([zblk, zpad, wuv[2 * p + 1]], axis=0)], axis=1)
        for p in range(N_HEADS_DSA // 2)])
    o1 = _dsa(hq, hr, klr, kid, tabs, wuv2, batch, seq, 128, _pick(seq, 512, 256))
    x3f, x3b = _matmul_ln([(o1, l1_w_o.astype(BF16))], x2f, l1_ln1_g, l1_ln1_b, 512, "l1_out_proj_ln")
    u1 = _matmul(x3b, l1_w_up.astype(BF16), BF16, bm, 512, "l1_ffn_up")
    x4f, _ = _ffn_down(u1, l1_conv_w, l1_conv_b, l1_w_down.astype(BF16), x3f, l1_ln2_g, l1_ln2_b,
                       seq, 256, "l1_ffn_down_ln")
    return x4f.reshape(batch, seq, d)
```

```python
import functools

import jax
import jax.numpy as jnp
from jax import lax
from jax.experimental import pallas as pl
from jax.experimental.pallas import tpu as pltpu

F32 = jnp.float32
BF16 = jnp.bfloat16

LANES = 128
VMEM_LIMIT = 56 * 1024 * 1024

HEAD_DIM = 64
N_HEADS_SB = 8
N_HEADS_FOX = 8
N_HEADS_DSA = 16
DSA_KEY_DIM = 256
DSA_ROPE = 64
DSA_V_HEAD = 64
IDX_HEADS = 16
IDX_DIM = 64
IDX_TOPK = 256
IDX_SCALE = (IDX_HEADS * IDX_DIM) ** -0.5
CHUNK = 64
ROPE_THETA = 500000.0
D_FF = 2816
LN_EPS = 1e-5
DEPTH = 2
ALPHA = (2 * DEPTH) ** 0.25
NEG = -1e30
INT_MIN = -(2 ** 31)
LOG2E = 1.4426950408889634
FFN_BM, FFN_BN = 512, 1408


def _cparams(*sem):
    return pltpu.CompilerParams(dimension_semantics=sem, vmem_limit_bytes=VMEM_LIMIT)


def _dot(a, b):
    return jnp.dot(a, b, preferred_element_type=F32)


def _dot_nt(a, b):
    return lax.dot_general(a, b, (((1,), (1,)), ((), ())), preferred_element_type=F32)


def _split3(x):
    x1 = x.astype(BF16)
    r1 = x - x1.astype(F32)
    x2 = r1.astype(BF16)
    x3 = (r1 - x2.astype(F32)).astype(BF16)
    return x1, x2, x3


def _mm_kernel(x_ref, w_ref, o_ref):
    o_ref[...] = _dot(x_ref[...], w_ref[...]).astype(o_ref.dtype)


def _matmul(x, w, out_dtype, bm, bn, name):
    m, k = x.shape
    n = w.shape[1]
    return pl.pallas_call(
        _mm_kernel,
        grid=(m // bm, n // bn),
        in_specs=[pl.BlockSpec((bm, k), lambda i, j: (i, 0)),
                  pl.BlockSpec((k, bn), lambda i, j: (0, j))],
        out_specs=pl.BlockSpec((bm, bn), lambda i, j: (i, j)),
        out_shape=jax.ShapeDtypeStruct((m, n), out_dtype),
        compiler_params=_cparams("parallel", "parallel"),
        name=name,
    )(x, w)


def _layer_norm_rows(y, g, b):
    mu = jnp.mean(y, axis=-1, keepdims=True)
    yc = y - mu
    var = jnp.mean(yc * yc, axis=-1, keepdims=True)
    return yc * lax.rsqrt(var + LN_EPS) * g + b


def _mm_ln_kernel(*refs, n_pairs):
    xs = refs[:n_pairs]
    ws = refs[n_pairs:2 * n_pairs]
    res_ref, g_ref, b_ref, of_ref, ob_ref = refs[2 * n_pairs:]
    acc = _dot(xs[0][...], ws[0][...])
    for p in range(1, n_pairs):
        acc = acc + _dot(xs[p][...], ws[p][...])
    out = _layer_norm_rows(ALPHA * res_ref[...] + acc, g_ref[...], b_ref[...])
    of_ref[...] = out
    ob_ref[...] = out.astype(BF16)


def _matmul_ln(pairs, res, g, b, bm, name):
    m, d = res.shape
    n_pairs = len(pairs)
    in_specs = ([pl.BlockSpec((bm, x.shape[1]), lambda i: (i, 0)) for x, _ in pairs]
                + [pl.BlockSpec(w.shape, lambda i: (0, 0)) for _, w in pairs]
                + [pl.BlockSpec((bm, d), lambda i: (i, 0)),
                   pl.BlockSpec((1, d), lambda i: (0, 0)),
                   pl.BlockSpec((1, d), lambda i: (0, 0))])
    return pl.pallas_call(
        functools.partial(_mm_ln_kernel, n_pairs=n_pairs),
        grid=(m // bm,),
        in_specs=in_specs,
        out_specs=[pl.BlockSpec((bm, d), lambda i: (i, 0)),
                   pl.BlockSpec((bm, d), lambda i: (i, 0))],
        out_shape=[jax.ShapeDtypeStruct((m, d), F32), jax.ShapeDtypeStruct((m, d), BF16)],
        compiler_params=_cparams("parallel"),
        name=name,
    )(*[x for x, _ in pairs], *[w for _, w in pairs], res, g.reshape(1, d), b.reshape(1, d))


HALO = 16


def _ffn_up_kernel(x_ref, xh_ref, wg_ref, wu_ref, cw_ref, cb_ref, o_ref, *, bm, seq):
    i = pl.program_id(0)
    starts_seq = lax.rem(i * bm, seq) == 0
    xh = xh_ref[...]
    xh = jnp.where(starts_seq, jnp.zeros_like(xh), xh)
    x = x_ref[...]
    gate = _dot(jnp.concatenate([xh, x], axis=0), wg_ref[...])
    up = _dot(x, wu_ref[...])
    g = gate[HALO:, :]
    g1 = pltpu.roll(gate, 1, 0)[HALO:, :]
    g2 = pltpu.roll(gate, 2, 0)[HALO:, :]
    cw = cw_ref[...]
    conv = cb_ref[...] + cw[0:1, :] * g2
    conv = conv + cw[1:2, :] * g1
    conv = conv + cw[2:3, :] * g
    cdf = 0.5 * (1.0 + jnp.tanh(0.7978845608028654 * (conv + 0.044715 * (conv * conv * conv))))
    o_ref[...] = ((conv * cdf) * up).astype(o_ref.dtype)


def _ffn_up(x, w_up, conv_w, conv_b, seq, bm, bn, name):
    m, d = x.shape
    f = w_up.shape[1] // 2
    nb = f // bn
    hb = bm // HALO
    return pl.pallas_call(
        functools.partial(_ffn_up_kernel, bm=bm, seq=seq),
        grid=(m // bm, nb),
        in_specs=[pl.BlockSpec((bm, d), lambda i, j: (i, 0)),
                  pl.BlockSpec((HALO, d), lambda i, j: (jnp.maximum(i * hb - 1, 0), 0)),
                  pl.BlockSpec((d, bn), lambda i, j: (0, j)),
                  pl.BlockSpec((d, bn), lambda i, j: (0, nb + j)),
                  pl.BlockSpec((3, bn), lambda i, j: (0, j)),
                  pl.BlockSpec((1, bn), lambda i, j: (0, j))],
        out_specs=pl.BlockSpec((bm, bn), lambda i, j: (i, j)),
        out_shape=jax.ShapeDtypeStruct((m, f), BF16),
        compiler_params=_cparams("parallel", "parallel"),
        name=name,
    )(x, x, w_up, w_up, conv_w, conv_b.reshape(1, f))


def _cum_kernel(f_ref, bias_ref, o_ref, *, rows_per_seq):
    x = f_ref[...] + bias_ref[...]
    ls = jnp.minimum(x, 0.0) - jnp.log(1.0 + jnp.exp(-jnp.abs(x)))
    r = x.shape[0]
    incl = (lax.broadcasted_iota(jnp.int32, (LANES, LANES), 0)
            <= lax.broadcasted_iota(jnp.int32, (LANES, LANES), 1)).astype(BF16)
    a1, a2, a3 = _split3(ls)
    cs = _dot(a1, incl) + _dot(a2, incl) + _dot(a3, incl)
    tot = jnp.broadcast_to(cs[:, LANES - 1:LANES], (r, LANES))
    ri = lax.broadcasted_iota(jnp.int32, (r, r), 0)
    ci = lax.broadcasted_iota(jnp.int32, (r, r), 1)
    shift = rows_per_seq.bit_length() - 1
    same_seq = (ri >> shift) == (ci >> shift)
    before = jnp.where(same_seq & (ci < ri), 1.0, 0.0).astype(BF16)
    t1, t2, t3 = _split3(tot)
    off = _dot(before, t1) + _dot(before, t2) + _dot(before, t3)
    o_ref[...] = cs + off


def _forget_cumsum(fl, b_f, batch, seq):
    nh = N_HEADS_FOX
    f = fl[:, :nh].reshape(batch, seq, nh).transpose(0, 2, 1).reshape(batch * nh * (seq // LANES), LANES)
    bias = jnp.broadcast_to(jnp.tile(b_f.astype(F32), batch)[:, None, None],
                            (batch * nh, seq // LANES, LANES)).reshape(f.shape)
    cum = pl.pallas_call(
        functools.partial(_cum_kernel, rows_per_seq=seq // LANES),
        out_shape=jax.ShapeDtypeStruct(f.shape, F32),
        compiler_params=pltpu.CompilerParams(vmem_limit_bytes=VMEM_LIMIT),
        name="forget_cumsum",
    )(f, bias)
    return cum.reshape(batch * nh, seq)


def _stack_heads(q, lane, scale):
    qf = q.astype(F32) * scale
    return jnp.concatenate([jnp.where(lane < HEAD_DIM, qf, 0.0), jnp.where(lane >= HEAD_DIM, qf, 0.0)],
                           axis=0).astype(BF16)


def _sb_kernel(q_ref, k_ref, v_ref, o_ref, acc_ref, c_ref, z_ref, *, tq, bk):
    i = pl.program_id(2)
    lane = lax.broadcasted_iota(jnp.int32, (1, LANES), 1)
    qs = _stack_heads(q_ref[...], lane, HEAD_DIM ** -0.5 * LOG2E)
    rj = lax.broadcasted_iota(jnp.int32, (2 * bk, bk), 0)
    cs_ = lax.broadcasted_iota(jnp.int32, (2 * bk, bk), 1)
    neg_suffix = jnp.where((rj & (bk - 1)) >= cs_, -1.0, 0.0).astype(BF16)
    row = lax.broadcasted_iota(jnp.int32, (tq, bk), 0) + i * tq
    col = lax.broadcasted_iota(jnp.int32, (tq, bk), 1)
    acc_ref[...] = jnp.zeros(acc_ref.shape, F32)
    c_ref[...] = jnp.zeros(c_ref.shape, F32)

    def scores(kb):
        start = pl.multiple_of(kb * bk, bk)
        return _dot_nt(qs, k_ref[pl.ds(start, bk), :])

    def step(kb, masked, nxt):
        z = z_ref[...]
        if nxt is not None:
            z_ref[...] = scores(nxt)
        start = pl.multiple_of(kb * bk, bk)
        v = v_ref[pl.ds(start, bk), :]
        sp = jnp.maximum(z, 0.0) + jnp.log(1.0 + jnp.exp2(-jnp.abs(z))) * LOG2E
        if masked:
            keep = (col + start) < row
            keep = jnp.concatenate([keep, keep], axis=0)
            sp = jnp.where(keep, sp, 0.0)
        hi = sp.astype(BF16)
        lo = (sp - hi.astype(F32)).astype(BF16)
        cs = _dot(jnp.concatenate([hi, lo], axis=1), neg_suffix)
        c = c_ref[...]
        w = jnp.exp2(z + cs + jnp.concatenate([c] * (bk // LANES), axis=1))
        if masked:
            w = jnp.where(keep, w, 0.0)
        acc_ref[...] += _dot(w.astype(BF16), v)
        c_ref[...] = c + jnp.broadcast_to(cs[:, 0:1], c.shape)

    per = tq // bk
    top = i * per + per - 1
    z_ref[...] = scores(top)
    for r in range(per - 1):
        step(top - r, True, top - r - 1)

    @pl.when(i == 0)
    def _():
        step(0, True, None)

    @pl.when(i > 0)
    def _():
        step(i * per, True, i * per - 1)

        def body(u, _):
            kb = i * per - 1 - 2 * u
            step(kb, False, kb - 1)
            step(kb - 1, False, jnp.maximum(kb - 2, 0))
            return 0

        lax.fori_loop(0, i * (per // 2), body, 0)

    o_ref[...] = jnp.where(lane < HEAD_DIM, acc_ref[0:tq, :], acc_ref[tq:2 * tq, :]).astype(o_ref.dtype)


def _sb_attention(h0, batch, seq, tq, bk):
    assert tq == 2 * bk and seq % tq == 0, (seq, tq, bk)
    npair = N_HEADS_SB // 2
    nq = seq // tq
    q_off, k_off, v_off = 0, npair, 2 * npair
    return pl.pallas_call(
        functools.partial(_sb_kernel, tq=tq, bk=bk),
        grid=(batch, npair, nq),
        in_specs=[pl.BlockSpec((tq, LANES), lambda b, p, i: (b * nq + i, q_off + p)),
                  pl.BlockSpec((seq, LANES), lambda b, p, i: (b, k_off + p)),
                  pl.BlockSpec((seq, LANES), lambda b, p, i: (b, v_off + p))],
        out_specs=pl.BlockSpec((tq, LANES), lambda b, p, i: (b * nq + i, p)),
        out_shape=jax.ShapeDtypeStruct((batch * seq, N_HEADS_SB * HEAD_DIM), BF16),
        scratch_shapes=[pltpu.VMEM((2 * tq, LANES), F32),
                        pltpu.VMEM((2 * tq, LANES), F32),
                        pltpu.VMEM((2 * tq, bk), F32)],
        compiler_params=_cparams("parallel", "parallel", "arbitrary"),
        name="sb_attention",
    )(h0, h0, h0)


def _fox_kernel(q_ref, k_ref, v_ref, cq_ref, ck_ref, o_ref, m_ref, acc_ref, z_ref, *, tq):
    i = pl.program_id(2)
    lane = lax.broadcasted_iota(jnp.int32, (1, LANES), 1)
    qs = _stack_heads(q_ref[...], lane, HEAD_DIM ** -0.5 * LOG2E)
    eye = (lax.broadcasted_iota(jnp.int32, (LANES, LANES), 0)
           == lax.broadcasted_iota(jnp.int32, (LANES, LANES), 1))
    cols = []
    for hh in range(2):
        cqr = cq_ref[hh] * LOG2E
        for r in range(tq // LANES):
            seg = jnp.broadcast_to(cqr[:, r * LANES:(r + 1) * LANES], (LANES, LANES))
            cols.append(jnp.sum(jnp.where(eye, seg, 0.0), axis=1, keepdims=True))
    cq = jnp.concatenate(cols, axis=0)
    row = lax.broadcasted_iota(jnp.int32, (tq, tq), 0)
    col = lax.broadcasted_iota(jnp.int32, (tq, tq), 1)
    causal = jnp.concatenate([col <= row, col <= row], axis=0)
    ones = jnp.ones((tq, LANES), BF16)
    m_ref[...] = jnp.full(m_ref.shape, NEG, F32)
    acc_ref[...] = jnp.zeros(acc_ref.shape, F32)

    def scores(kb):
        start = pl.multiple_of(kb * tq, tq)
        k = k_ref[pl.ds(start, tq), :]
        ck = jnp.concatenate([jnp.broadcast_to(ck_ref[0, :, pl.ds(start, tq)] * LOG2E, (tq, tq)),
                              jnp.broadcast_to(ck_ref[1, :, pl.ds(start, tq)] * LOG2E, (tq, tq))], axis=0)
        return (_dot_nt(qs, k) + cq) - ck

    def step(kb, masked, nxt):
        lg = z_ref[...]
        if nxt is not None:
            z_ref[...] = scores(nxt)
        if masked:
            lg = jnp.where(causal, lg, NEG)
        start = pl.multiple_of(kb * tq, tq)
        v = v_ref[pl.ds(start, tq), :]
        m_old = m_ref[...]
        m_new = jnp.maximum(m_old, jnp.max(lg, axis=1, keepdims=True))
        p = jnp.exp2(lg - jnp.concatenate([m_new] * (tq // LANES), axis=1))
        a = jnp.exp2(m_old - m_new)
        acc_ref[...] = (jnp.concatenate([a, a], axis=1) * acc_ref[...]
                        + _dot(p.astype(BF16), jnp.concatenate([v, ones], axis=1)))
        m_ref[...] = m_new

    z_ref[...] = scores(i)

    @pl.when(i == 0)
    def _():
        step(0, True, None)

    @pl.when(i > 0)
    def _():
        step(i, True, i - 1)

        def body(t, _):
            step(i - t, False, jnp.maximum(i - t - 1, 0))
            return 0

        lax.fori_loop(1, i + 1, body, 0)

    out = acc_ref[:, 0:LANES] / acc_ref[:, LANES:2 * LANES]
    o_ref[...] = jnp.where(lane < HEAD_DIM, out[0:tq], out[tq:2 * tq]).astype(o_ref.dtype)


def _fox_attention(h0, cum, batch, seq, tq):
    npair = N_HEADS_FOX // 2
    nq = seq // tq
    base = 3 * (N_HEADS_SB // 2)
    q_off, k_off, v_off = base, base + npair, base + 2 * npair
    cum_k = cum.reshape(batch * N_HEADS_FOX, 1, seq)
    return pl.pallas_call(
        functools.partial(_fox_kernel, tq=tq),
        grid=(batch, npair, nq),
        in_specs=[pl.BlockSpec((tq, LANES), lambda b, p, i: (b * nq + i, q_off + p)),
                  pl.BlockSpec((seq, LANES), lambda b, p, i: (b, k_off + p)),
                  pl.BlockSpec((seq, LANES), lambda b, p, i: (b, v_off + p)),
                  pl.BlockSpec((2, 1, tq), lambda b, p, i: (b * npair + p, 0, i)),
                  pl.BlockSpec((2, 1, seq), lambda b, p, i: (b * npair + p, 0, 0))],
        out_specs=pl.BlockSpec((tq, LANES), lambda b, p, i: (b * nq + i, p)),
        out_shape=jax.ShapeDtypeStruct((batch * seq, N_HEADS_FOX * HEAD_DIM), BF16),
        scratch_shapes=[pltpu.VMEM((2 * tq, LANES), F32),
                        pltpu.VMEM((2 * tq, 2 * LANES), F32),
                        pltpu.VMEM((2 * tq, tq), F32)],
        compiler_params=_cparams("parallel", "parallel", "arbitrary"),
        name="fox_attention",
    )(h0, h0, h0, cum_k, cum_k)


def _rope_q(x, cos_t, sin_t, lane):
    xs = jnp.where(lane < DSA_ROPE // 2, pltpu.roll(x, LANES - DSA_ROPE // 2, 1), pltpu.roll(x, DSA_ROPE // 2, 1))
    return x * cos_t + xs * sin_t


def _rope_idx(x, cos_t, sin_t, lane):
    half = IDX_DIM // 8
    xs = jnp.where((lane & (IDX_DIM - 1)) < half, pltpu.roll(x, LANES - half, 1), pltpu.roll(x, half, 1))
    return x * cos_t + xs * sin_t


def _dsa_prep_kernel(kl_ref, km_ref, cq_ref, sq_ref, ci_ref, si_ref, klo_ref, kio_ref):
    lane = lax.broadcasted_iota(jnp.int32, (1, LANES), 1)
    klo_ref[:, 0:LANES] = _rope_q(kl_ref[:, 0:LANES], cq_ref[...], sq_ref[...], lane).astype(BF16)
    klo_ref[:, LANES:2 * LANES] = kl_ref[:, LANES:2 * LANES].astype(BF16)
    yr = _rope_idx(km_ref[...], ci_ref[...], si_ref[...], lane)
    kio_ref[...] = jnp.where(lane < IDX_DIM, yr, pltpu.roll(yr, IDX_DIM, 1)).astype(BF16)


def _dsa_prep(hr, tabs, rb):
    m = hr.shape[0]
    kl_blk = IDX_HEADS * IDX_DIM // (2 * LANES)
    km_blk = (IDX_HEADS * IDX_DIM + DSA_KEY_DIM) // LANES
    tspec = pl.BlockSpec((rb, LANES), lambda i: (i, 0))
    return pl.pallas_call(
        _dsa_prep_kernel,
        grid=(m // rb,),
        in_specs=[pl.BlockSpec((rb, 2 * LANES), lambda i: (i, kl_blk)),
                  pl.BlockSpec((rb, LANES), lambda i: (i, km_blk)),
                  tspec, tspec, tspec, tspec],
        out_specs=[pl.BlockSpec((rb, 2 * LANES), lambda i: (i, 0)),
                   pl.BlockSpec((rb, LANES), lambda i: (i, 0))],
        out_shape=[jax.ShapeDtypeStruct((m, DSA_KEY_DIM), BF16), jax.ShapeDtypeStruct((m, LANES), BF16)],
        compiler_params=_cparams("parallel"),
        name="dsa_key_prep",
    )(hr, hr, *tabs)


def _ordered_to_f32(o):
    return lax.bitcast_convert_type(jnp.where(o >= 0, o, o ^ jnp.int32(0x7FFFFFFF)), F32)


def _f32_to_ordered(x):
    b = lax.bitcast_convert_type(x, jnp.int32)
    return jnp.where(b >= 0, b, b ^ jnp.int32(0x7FFFFFFF))


def _dsa_kernel(q_ref, qi_ref, km_ref, cq_ref, sq_ref, ci_ref, si_ref, kl_ref, ki_ref, wuv_ref, o_ref,
                sc_ref, qs_ref, qis_ref, wb_ref, m_ref, acc_ref, z_ref, *, tq, bk, topk, n_grp):
    i = pl.program_id(1)
    nh = N_HEADS_DSA
    lane = lax.broadcasted_iota(jnp.int32, (1, LANES), 1)
    scale = DSA_KEY_DIM ** -0.5 * LOG2E
    gr = nh * tq // n_grp
    rep = bk // LANES

    cq, sq = cq_ref[...], sq_ref[...]
    for h in range(nh):
        c0 = h * DSA_KEY_DIM
        xr = _rope_q(q_ref[:, c0:c0 + LANES].astype(F32), cq, sq, lane) * scale
        qs_ref[h * tq:(h + 1) * tq, 0:LANES] = xr.astype(BF16)
        qs_ref[h * tq:(h + 1) * tq, LANES:2 * LANES] = (
            q_ref[:, c0 + LANES:c0 + 2 * LANES].astype(F32) * scale).astype(BF16)
    ci, si = ci_ref[...], si_ref[...]
    km = km_ref[...]
    for p in range(IDX_HEADS // 2):
        xr = _rope_idx(qi_ref[:, p * LANES:(p + 1) * LANES], ci, si, lane)
        for hh in range(2):
            h = 2 * p + hh
            qis_ref[h * tq:(h + 1) * tq, :] = jnp.where((lane >> 6) == hh, xr, 0.0).astype(BF16)
            wb_ref[h * tq:(h + 1) * tq, :] = jnp.broadcast_to(km[:, IDX_DIM + h:IDX_DIM + h + 1], (tq, LANES))

    nkb = lax.div((i + 1) * tq + bk - 1, bk)
    t_abs = i * tq + lax.broadcasted_iota(jnp.int32, (tq, 1), 0)
    vis_lim = ((t_abs >> 6) + 1) << 6
    col0 = lax.broadcasted_iota(jnp.int32, (tq, bk), 1)

    def score_body(kb, _):
        start = pl.multiple_of(kb * bk, bk)
        kd = ki_ref[pl.ds(start, bk), :]
        s = jnp.zeros((tq, bk), F32)
        for g in range(n_grp):
            rows = slice(g * gr, (g + 1) * gr)
            wb = jnp.concatenate([wb_ref[rows, :]] * rep, axis=1)
            lg = jnp.maximum(_dot_nt(qis_ref[rows, :], kd), 0.0) * wb
            for h in range(gr // tq):
                s = s + lg[h * tq:(h + 1) * tq]
        s = s * IDX_SCALE
        s = jnp.where(s == 0.0, 0.0, s)
        s = jnp.where(col0 + start < vis_lim, s, -jnp.inf)
        sc_ref[:, pl.ds(start, bk)] = s
        return 0

    lax.fori_loop(0, nkb, score_body, 0)

    def count(pred):
        def body(kb, c):
            start = pl.multiple_of(kb * bk, bk)
            hit = jnp.where(pred(sc_ref[:, pl.ds(start, bk)], col0 + start), 1.0, 0.0)
            for j in range(rep):
                c = c + hit[:, j * LANES:(j + 1) * LANES]
            return c
        c = lax.fori_loop(0, nkb, body, jnp.zeros((tq, LANES), F32))
        return jnp.sum(c, axis=1, keepdims=True)

    def bit_body(step, lo):
        cand = lo + lax.shift_left(jnp.int32(1), 31 - step)
        cf = _ordered_to_f32(cand)
        cnt = count(lambda s, c: s >= cf)
        return jnp.where(cnt >= topk, cand, lo)

    few = vis_lim <= topk
    first_bit = jnp.int32(0)
    lo0 = jnp.full((tq, 1), INT_MIN, jnp.int32)
    if topk == 2 * LANES:
        def top2_body(kb, carry):
            m1, m2 = carry
            start = pl.multiple_of(kb * bk, bk)
            t = sc_ref[:, pl.ds(start, bk)]
            for j in range(rep):
                x = t[:, j * LANES:(j + 1) * LANES]
                m2 = jnp.maximum(m2, jnp.minimum(m1, x))
                m1 = jnp.maximum(m1, x)
            return m1, m2

        ninf = jnp.full((tq, LANES), -jnp.inf, F32)
        _, m2 = lax.fori_loop(0, nkb, top2_body, (ninf, ninf))
        o_lo = _f32_to_ordered(jnp.min(m2, axis=1, keepdims=True))
        o_hi = _f32_to_ordered(jnp.max(m2, axis=1, keepdims=True))
        shared = jnp.where(few, 32, lax.clz(o_lo ^ o_hi))
        first_bit = jnp.min(shared)
        keep_mask = lax.shift_left(jnp.int32(-1), jnp.minimum(32 - first_bit, 31))
        lo0 = jnp.where(first_bit == 0, jnp.int32(INT_MIN), o_lo & keep_mask)
    lo = lax.fori_loop(first_bit, 32, bit_body, lo0)
    thr = jnp.where(few, -jnp.inf, _ordered_to_f32(lo))
    cnt_gt = count(lambda s, c: s > thr)
    cnt_ge = count(lambda s, c: s >= thr)
    need = topk - cnt_gt
    tie = jnp.logical_and(cnt_ge > topk, jnp.logical_not(few))

    def tie_search(_):
        def jb(step, x):
            cand = x + lax.shift_left(jnp.int32(1), 30 - step)
            cnt = count(lambda s, c: jnp.logical_and(s == thr, c < cand))
            return jnp.where(cnt < need, cand, x)
        return lax.fori_loop(0, 31, jb, jnp.zeros((tq, 1), jnp.int32))

    any_tie = jnp.max(jnp.where(tie, 1.0, 0.0)) > 0.0
    jlim = lax.cond(any_tie, tie_search, lambda _: jnp.zeros((tq, 1), jnp.int32), 0)
    jlim = jnp.where(tie, jlim, jnp.int32(2 ** 30))

    def bias_body(kb, _):
        start = pl.multiple_of(kb * bk, bk)
        s = sc_ref[:, pl.ds(start, bk)]
        c = col0 + start
        keep = jnp.logical_or(s > thr, jnp.logical_and(s == thr, c <= jlim))
        keep = jnp.logical_and(keep, c < vis_lim)
        sc_ref[:, pl.ds(start, bk)] = jnp.where(keep, 0.0, NEG)
        return 0

    lax.fori_loop(0, nkb, bias_body, 0)

    m_ref[...] = jnp.full(m_ref.shape, NEG, F32)
    acc_ref[...] = jnp.zeros(acc_ref.shape, F32)
    lane2 = lax.broadcasted_iota(jnp.int32, (1, DSA_KEY_DIM), 1)

    def qk(kb):
        start = pl.multiple_of(kb * bk, bk)
        kl = kl_ref[pl.ds(start, bk), :]
        for g in range(n_grp):
            rows = slice(g * gr, (g + 1) * gr)
            z_ref[rows, :] = _dot_nt(qs_ref[rows, :], kl)

    def attn_step(kb, prefetch):
        start = pl.multiple_of(kb * bk, bk)
        kl = kl_ref[pl.ds(start, bk), :]
        vl = jnp.where(lane2 < DSA_ROPE, jnp.ones_like(kl), kl)
        bias = sc_ref[:, pl.ds(start, bk)]
        bias_g = jnp.concatenate([bias] * (gr // tq), axis=0)
        if prefetch:
            nstart = pl.multiple_of((kb + 1) * bk, bk)
            kn = kl_ref[pl.ds(nstart, bk), :]
        for g in range(n_grp):
            rows = slice(g * gr, (g + 1) * gr)
            lg = z_ref[rows, :] + bias_g
            if prefetch:
                z_ref[rows, :] = _dot_nt(qs_ref[rows, :], kn)
            m_old = m_ref[rows, :]
            m_new = jnp.maximum(m_old, jnp.max(lg, axis=1, keepdims=True))
            p = jnp.exp2(lg - jnp.concatenate([m_new] * rep, axis=1))
            a = jnp.exp2(m_old - m_new)
            acc_ref[rows, :] = jnp.concatenate([a, a], axis=1) * acc_ref[rows, :] + _dot(p.astype(BF16), vl)
            m_ref[rows, :] = m_new

    qk(0)

    def attn_body(kb, _):
        attn_step(kb, True)
        return 0

    lax.fori_loop(0, nkb - 1, attn_body, 0)
    attn_step(nkb - 1, False)

    for p in range(nh // 2):
        parts = []
        for hh in range(2):
            rows = slice((2 * p + hh) * tq, (2 * p + hh + 1) * tq)
            acc = acc_ref[rows, :]
            inv = 1.0 / jnp.broadcast_to(acc[:, 0:1], acc.shape)
            parts.append((acc * inv).astype(BF16))
        ctx = jnp.concatenate(parts, axis=1)
        o_ref[:, p * LANES:(p + 1) * LANES] = _dot(ctx, wuv_ref[p]).astype(o_ref.dtype)


def _dsa(hq, hr, klr, kid, tabs, wuv2, batch, seq, tq, bk):
    nq = seq // tq
    nh = N_HEADS_DSA
    km_blk = (IDX_HEADS * IDX_DIM + DSA_KEY_DIM) // LANES
    topk = min(IDX_TOPK, seq // 4)
    tspec = pl.BlockSpec((tq, LANES), lambda b, i: (b * nq + i, 0))
    return pl.pallas_call(
        functools.partial(_dsa_kernel, tq=tq, bk=bk, topk=topk, n_grp=4),
        grid=(batch, nq),
        in_specs=[pl.BlockSpec((tq, nh * DSA_KEY_DIM), lambda b, i: (b * nq + i, 0)),
                  pl.BlockSpec((tq, IDX_HEADS * IDX_DIM), lambda b, i: (b * nq + i, 0)),
                  pl.BlockSpec((tq, LANES), lambda b, i: (b * nq + i, km_blk)),
                  tspec, tspec, tspec, tspec,
                  pl.BlockSpec((seq, DSA_KEY_DIM), lambda b, i: (b, 0)),
                  pl.BlockSpec((seq, LANES), lambda b, i: (b, 0)),
                  pl.BlockSpec(wuv2.shape, lambda b, i: (0, 0, 0))],
        out_specs=pl.BlockSpec((tq, nh * DSA_V_HEAD), lambda b, i: (b * nq + i, 0)),
        out_shape=jax.ShapeDtypeStruct((batch * seq, nh * DSA_V_HEAD), BF16),
        scratch_shapes=[pltpu.VMEM((tq, seq), F32),
                        pltpu.VMEM((nh * tq, DSA_KEY_DIM), BF16),
                        pltpu.VMEM((nh * tq, LANES), BF16),
                        pltpu.VMEM((nh * tq, LANES), F32),
                        pltpu.VMEM((nh * tq, LANES), F32),
                        pltpu.VMEM((nh * tq, DSA_KEY_DIM), F32),
                        pltpu.VMEM((nh * tq, bk), F32)],
        compiler_params=_cparams("parallel", "arbitrary"),
        name="dsa_attention",
    )(hq, hr, hr, *tabs, klr, kid, wuv2)


def _rope_tables(positions):
    pos = positions.astype(F32).reshape(-1, 1)
    n = pos.shape[0]

    def cs(half):
        freqs = ROPE_THETA ** (-jnp.arange(half, dtype=F32) / half)
        ang = pos * freqs
        return jnp.cos(ang), jnp.sin(ang)

    cq, sq = cs(DSA_ROPE // 2)
    ones, zeros = jnp.ones((n, LANES - DSA_ROPE), F32), jnp.zeros((n, LANES - DSA_ROPE), F32)
    cos_q = jnp.concatenate([cq, cq, ones], axis=1)
    sin_q = jnp.concatenate([-sq, sq, zeros], axis=1)
    ci, si = cs(IDX_DIM // 8)
    pad = IDX_DIM - IDX_DIM // 4
    ci64 = jnp.concatenate([ci, ci, jnp.ones((n, pad), F32)], axis=1)
    si64 = jnp.concatenate([-si, si, jnp.zeros((n, pad), F32)], axis=1)
    return cos_q, sin_q, jnp.concatenate([ci64, ci64], axis=1), jnp.concatenate([si64, si64], axis=1)


def _pick(n, *cands):
    for c in cands:
        if n % c == 0:
            return c
    return n


def kernel(x, positions, l0_w_in, l0_b_f, l0_w_o, l0_ln1_g, l0_ln1_b, l0_w_up, l0_conv_w, l0_conv_b, l0_w_down, l0_ln2_g, l0_ln2_b, l1_w_in, l1_w_uv, l1_w_o, l1_ln1_g, l1_ln1_b, l1_w_up, l1_conv_w, l1_conv_b, l1_w_down, l1_ln2_g, l1_ln2_b):
    batch, seq, d = x.shape
    m = batch * seq
    xf = x.reshape(m, d).astype(F32)
    xb = xf.astype(BF16)
    bm = _pick(m, 1024, 512, 256)
    tq_a = _pick(seq, 512, 256, 128)
    bk_sb = _pick(tq_a, 256, 128)

    n_qkv = 3 * (N_HEADS_SB + N_HEADS_FOX) * HEAD_DIM
    h0 = _matmul(xb, l0_w_in[:, :n_qkv].astype(BF16), BF16, bm, 1536, "l0_in_proj")
    w_f = jnp.pad(l0_w_in[:, n_qkv:], ((0, 0), (0, LANES - N_HEADS_FOX))).astype(BF16)
    fl = _matmul(xb, w_f, F32, bm, LANES, "l0_forget_proj")
    cum = _forget_cumsum(fl, l0_b_f, batch, seq)
    oa = _sb_attention(h0, batch, seq, tq_a, bk_sb)
    ob = _fox_attention(h0, cum, batch, seq, tq_a)
    wo = l0_w_o.astype(BF16)
    na = N_HEADS_SB * HEAD_DIM
    x1f, x1b = _matmul_ln([(oa, wo[:na]), (ob, wo[na:])], xf, l0_ln1_g, l0_ln1_b, 512, "l0_out_proj_ln")
    a0 = _ffn_up(x1b, l0_w_up.astype(BF16), l0_conv_w, l0_conv_b, seq, FFN_BM, FFN_BN, "l0_ffn_up_act")
    x2f, x2b = _matmul_ln([(a0, l0_w_down.astype(BF16))], x1f, l0_ln2_g, l0_ln2_b, 512, "l0_ffn_down_ln")

    c0 = N_HEADS_DSA * DSA_KEY_DIM
    c2 = c0 + DSA_KEY_DIM
    c3 = c2 + IDX_HEADS * IDX_DIM
    n_in = l1_w_in.shape[1]
    hq = _matmul(x2b, l1_w_in[:, :c0].astype(BF16), BF16, bm, 1024, "l1_q_proj")
    w_rest = jnp.concatenate([l1_w_in[:, c2:c3], l1_w_in[:, c0:c2], l1_w_in[:, c3:],
                              jnp.zeros((d, LANES - (n_in - c3)), F32)], axis=1).astype(BF16)
    hr = _matmul(x2b, w_rest, F32, 512, w_rest.shape[1], "l1_kidx_proj")
    tabs = _rope_tables(positions)
    klr, kid = _dsa_prep(hr, tabs, 512)
    wuv = l1_w_uv.astype(BF16)
    zpad = jnp.zeros((DSA_ROPE, DSA_V_HEAD), BF16)
    zblk = jnp.zeros((DSA_KEY_DIM, DSA_V_HEAD), BF16)
    wuv2 = jnp.stack([
        jnp.concatenate([jnp.concatenate([zpad, wuv[2 * p], zblk], axis=0),
                         jnp.concatenate([zblk, zpad, wuv[2 * p + 1]], axis=0)], axis=1)
        for p in range(N_HEADS_DSA // 2)])
    o1 = _dsa(hq, hr, klr, kid, tabs, wuv2, batch, seq, 128, _pick(seq, 512, 256))
    x3f, x3b = _matmul_ln([(o1, l1_w_o.astype(BF16))], x2f, l1_ln1_g, l1_ln1_b, 512, "l1_out_proj_ln")
    a1 = _ffn_up(x3b, l1_w_up.astype(BF16), l1_conv_w, l1_conv_b, seq, FFN_BM, FFN_BN, "l1_ffn_up_act")
    x4f, _ = _matmul_ln([(a1, l1_w_down.astype(BF16))], x3f, l1_ln2_g, l1_ln2_b, 512, "l1_ffn_down_ln")
    return x4f.reshape(batch, seq, d)
```

```python
import functools

import jax
import jax.numpy as jnp
from jax import lax
from jax.experimental import pallas as pl
from jax.experimental.pallas import tpu as pltpu

F32 = jnp.float32
BF16 = jnp.bfloat16

LANES = 128
VMEM_LIMIT = 56 * 1024 * 1024

HEAD_DIM = 64
N_HEADS_SB = 8
N_HEADS_FOX = 8
N_HEADS_DSA = 16
DSA_KEY_DIM = 256
DSA_ROPE = 64
DSA_V_HEAD = 64
IDX_HEADS = 16
IDX_DIM = 64
IDX_TOPK = 256
IDX_SCALE = (IDX_HEADS * IDX_DIM) ** -0.5
CHUNK = 64
ROPE_THETA = 500000.0
D_FF = 2816
LN_EPS = 1e-5
DEPTH = 2
ALPHA = (2 * DEPTH) ** 0.25
NEG = -1e30
INT_MIN = -(2 ** 31)
LOG2E = 1.4426950408889634
FFN_BM, FFN_BN = 512, 1408


def _cparams(*sem):
    return pltpu.CompilerParams(dimension_semantics=sem, vmem_limit_bytes=VMEM_LIMIT)


def _dot(a, b):
    return jnp.dot(a, b, preferred_element_type=F32)


def _dot_nt(a, b):
    return lax.dot_general(a, b, (((1,), (1,)), ((), ())), preferred_element_type=F32)


def _split3(x):
    x1 = x.astype(BF16)
    r1 = x - x1.astype(F32)
    x2 = r1.astype(BF16)
    x3 = (r1 - x2.astype(F32)).astype(BF16)
    return x1, x2, x3


def _mm_kernel(x_ref, w_ref, o_ref):
    o_ref[...] = _dot(x_ref[...], w_ref[...]).astype(o_ref.dtype)


def _matmul(x, w, out_dtype, bm, bn, name):
    m, k = x.shape
    n = w.shape[1]
    return pl.pallas_call(
        _mm_kernel,
        grid=(m // bm, n // bn),
        in_specs=[pl.BlockSpec((bm, k), lambda i, j: (i, 0)),
                  pl.BlockSpec((k, bn), lambda i, j: (0, j))],
        out_specs=pl.BlockSpec((bm, bn), lambda i, j: (i, j)),
        out_shape=jax.ShapeDtypeStruct((m, n), out_dtype),
        compiler_params=_cparams("parallel", "parallel"),
        name=name,
    )(x, w)


def _layer_norm_rows(y, g, b):
    mu = jnp.mean(y, axis=-1, keepdims=True)
    yc = y - mu
    var = jnp.mean(yc * yc, axis=-1, keepdims=True)
    return yc * lax.rsqrt(var + LN_EPS) * g + b


def _mm_ln_kernel(*refs, n_pairs):
    xs = refs[:n_pairs]
    ws = refs[n_pairs:2 * n_pairs]
    res_ref, g_ref, b_ref, of_ref, ob_ref = refs[2 * n_pairs:]
    acc = _dot(xs[0][...], ws[0][...])
    for p in range(1, n_pairs):
        acc = acc + _dot(xs[p][...], ws[p][...])
    out = _layer_norm_rows(ALPHA * res_ref[...] + acc, g_ref[...], b_ref[...])
    of_ref[...] = out
    ob_ref[...] = out.astype(BF16)


def _matmul_ln(pairs, res, g, b, bm, name):
    m, d = res.shape
    n_pairs = len(pairs)
    in_specs = ([pl.BlockSpec((bm, x.shape[1]), lambda i: (i, 0)) for x, _ in pairs]
                + [pl.BlockSpec(w.shape, lambda i: (0, 0)) for _, w in pairs]
                + [pl.BlockSpec((bm, d), lambda i: (i, 0)),
                   pl.BlockSpec((1, d), lambda i: (0, 0)),
                   pl.BlockSpec((1, d), lambda i: (0, 0))])
    return pl.pallas_call(
        functools.partial(_mm_ln_kernel, n_pairs=n_pairs),
        grid=(m // bm,),
        in_specs=in_specs,
        out_specs=[pl.BlockSpec((bm, d), lambda i: (i, 0)),
                   pl.BlockSpec((bm, d), lambda i: (i, 0))],
        out_shape=[jax.ShapeDtypeStruct((m, d), F32), jax.ShapeDtypeStruct((m, d), BF16)],
        compiler_params=_cparams("parallel"),
        name=name,
    )(*[x for x, _ in pairs], *[w for _, w in pairs], res, g.reshape(1, d), b.reshape(1, d))


HALO = 16


def _ffn_up_kernel(x_ref, xh_ref, wg_ref, wu_ref, cw_ref, cb_ref, o_ref, *, bm, seq):
    i = pl.program_id(0)
    starts_seq = lax.rem(i * bm, seq) == 0
    xh = xh_ref[...]
    xh = jnp.where(starts_seq, jnp.zeros_like(xh), xh)
    x = x_ref[...]
    gate = _dot(jnp.concatenate([xh, x], axis=0), wg_ref[...])
    up = _dot(x, wu_ref[...])
    g = gate[HALO:, :]
    g1 = pltpu.roll(gate, 1, 0)[HALO:, :]
    g2 = pltpu.roll(gate, 2, 0)[HALO:, :]
    cw = cw_ref[...]
    conv = cb_ref[...] + cw[0:1, :] * g2
    conv = conv + cw[1:2, :] * g1
    conv = conv + cw[2:3, :] * g
    cdf = 0.5 * (1.0 + jnp.tanh(0.7978845608028654 * (conv + 0.044715 * (conv * conv * conv))))
    o_ref[...] = ((conv * cdf) * up).astype(o_ref.dtype)


def _ffn_up(x, w_up, conv_w, conv_b, seq, bm, bn, name):
    m, d = x.shape
    f = w_up.shape[1] // 2
    nb = f // bn
    hb = bm // HALO
    return pl.pallas_call(
        functools.partial(_ffn_up_kernel, bm=bm, seq=seq),
        grid=(m // bm, nb),
        in_specs=[pl.BlockSpec((bm, d), lambda i, j: (i, 0)),
                  pl.BlockSpec((HALO, d), lambda i, j: (jnp.maximum(i * hb - 1, 0), 0)),
                  pl.BlockSpec((d, bn), lambda i, j: (0, j)),
                  pl.BlockSpec((d, bn), lambda i, j: (0, nb + j)),
                  pl.BlockSpec((3, bn), lambda i, j: (0, j)),
                  pl.BlockSpec((1, bn), lambda i, j: (0, j))],
        out_specs=pl.BlockSpec((bm, bn), lambda i, j: (i, j)),
        out_shape=jax.ShapeDtypeStruct((m, f), BF16),
        compiler_params=_cparams("parallel", "parallel"),
        name=name,
    )(x, x, w_up, w_up, conv_w, conv_b.reshape(1, f))


def _cum_kernel(f_ref, bias_ref, o_ref, *, rows_per_seq):
    x = f_ref[...] + bias_ref[...]
    ls = jnp.minimum(x, 0.0) - jnp.log(1.0 + jnp.exp(-jnp.abs(x)))
    r = x.shape[0]
    incl = (lax.broadcasted_iota(jnp.int32, (LANES, LANES), 0)
            <= lax.broadcasted_iota(jnp.int32, (LANES, LANES), 1)).astype(BF16)
    a1, a2, a3 = _split3(ls)
    cs = _dot(a1, incl) + _dot(a2, incl) + _dot(a3, incl)
    tot = jnp.broadcast_to(cs[:, LANES - 1:LANES], (r, LANES))
    ri = lax.broadcasted_iota(jnp.int32, (r, r), 0)
    ci = lax.broadcasted_iota(jnp.int32, (r, r), 1)
    shift = rows_per_seq.bit_length() - 1
    same_seq = (ri >> shift) == (ci >> shift)
    before = jnp.where(same_seq & (ci < ri), 1.0, 0.0).astype(BF16)
    t1, t2, t3 = _split3(tot)
    off = _dot(before, t1) + _dot(before, t2) + _dot(before, t3)
    o_ref[...] = cs + off


def _forget_cumsum(fl, b_f, batch, seq):
    nh = N_HEADS_FOX
    f = fl[:, :nh].reshape(batch, seq, nh).transpose(0, 2, 1).reshape(batch * nh * (seq // LANES), LANES)
    bias = jnp.broadcast_to(jnp.tile(b_f.astype(F32), batch)[:, None, None],
                            (batch * nh, seq // LANES, LANES)).reshape(f.shape)
    cum = pl.pallas_call(
        functools.partial(_cum_kernel, rows_per_seq=seq // LANES),
        out_shape=jax.ShapeDtypeStruct(f.shape, F32),
        compiler_params=pltpu.CompilerParams(vmem_limit_bytes=VMEM_LIMIT),
        name="forget_cumsum",
    )(f, bias)
    return cum.reshape(batch * nh, seq)


def _neg_abs(x):
    return lax.bitcast_convert_type(lax.bitcast_convert_type(x, jnp.int32) | jnp.int32(INT_MIN), F32)


def _stack_heads(q, lane, scale):
    qf = q.astype(F32) * scale
    return jnp.concatenate([jnp.where(lane < HEAD_DIM, qf, 0.0), jnp.where(lane >= HEAD_DIM, qf, 0.0)],
                           axis=0).astype(BF16)


def _sb_kernel(q_ref, k_ref, v_ref, o_ref, acc_ref, c_ref, z_ref, *, tq, bk):
    i = pl.program_id(2)
    lane = lax.broadcasted_iota(jnp.int32, (1, LANES), 1)
    qs = _stack_heads(q_ref[...], lane, HEAD_DIM ** -0.5 * LOG2E)
    rj = lax.broadcasted_iota(jnp.int32, (2 * bk, bk), 0)
    cs_ = lax.broadcasted_iota(jnp.int32, (2 * bk, bk), 1)
    neg_suffix = jnp.where((rj & (bk - 1)) >= cs_, -1.0, 0.0).astype(BF16)
    row = lax.broadcasted_iota(jnp.int32, (tq, bk), 0) + i * tq
    col = lax.broadcasted_iota(jnp.int32, (tq, bk), 1)
    acc_ref[...] = jnp.zeros(acc_ref.shape, F32)
    c_ref[...] = jnp.zeros(c_ref.shape, F32)

    def scores(kb):
        start = pl.multiple_of(kb * bk, bk)
        return _dot_nt(qs, k_ref[pl.ds(start, bk), :])

    def step(kb, masked, nxt):
        z = z_ref[...]
        if nxt is not None:
            z_ref[...] = scores(nxt)
        start = pl.multiple_of(kb * bk, bk)
        v = v_ref[pl.ds(start, bk), :]
        sp = jnp.maximum(z, 0.0) + jnp.log(1.0 + jnp.exp2(_neg_abs(z))) * LOG2E
        if masked:
            keep = (col + start) < row
            keep = jnp.concatenate([keep, keep], axis=0)
            sp = jnp.where(keep, sp, 0.0)
        hi = sp.astype(BF16)
        lo = (sp - hi.astype(F32)).astype(BF16)
        cs = _dot(jnp.concatenate([hi, lo], axis=1), neg_suffix)
        c = c_ref[...]
        w = jnp.exp2(z + cs + jnp.concatenate([c] * (bk // LANES), axis=1))
        if masked:
            w = jnp.where(keep, w, 0.0)
        acc_ref[...] += _dot(w.astype(BF16), v)
        c_ref[...] = c + jnp.broadcast_to(cs[:, 0:1], c.shape)

    per = tq // bk
    top = i * per + per - 1
    z_ref[...] = scores(top)
    for r in range(per - 1):
        step(top - r, True, top - r - 1)

    @pl.when(i == 0)
    def _():
        step(0, True, None)

    @pl.when(i > 0)
    def _():
        step(i * per, True, i * per - 1)

        def body(u, _):
            kb = i * per - 1 - 2 * u
            step(kb, False, kb - 1)
            step(kb - 1, False, jnp.maximum(kb - 2, 0))
            return 0

        lax.fori_loop(0, i * (per // 2), body, 0)

    o_ref[...] = jnp.where(lane < HEAD_DIM, acc_ref[0:tq, :], acc_ref[tq:2 * tq, :]).astype(o_ref.dtype)


def _sb_attention(h0, batch, seq, tq, bk):
    assert tq == 2 * bk and seq % tq == 0, (seq, tq, bk)
    npair = N_HEADS_SB // 2
    nq = seq // tq
    q_off, k_off, v_off = 0, npair, 2 * npair
    return pl.pallas_call(
        functools.partial(_sb_kernel, tq=tq, bk=bk),
        grid=(batch, npair, nq),
        in_specs=[pl.BlockSpec((tq, LANES), lambda b, p, i: (b * nq + i, q_off + p)),
                  pl.BlockSpec((seq, LANES), lambda b, p, i: (b, k_off + p)),
                  pl.BlockSpec((seq, LANES), lambda b, p, i: (b, v_off + p))],
        out_specs=pl.BlockSpec((tq, LANES), lambda b, p, i: (b * nq + i, p)),
        out_shape=jax.ShapeDtypeStruct((batch * seq, N_HEADS_SB * HEAD_DIM), BF16),
        scratch_shapes=[pltpu.VMEM((2 * tq, LANES), F32),
                        pltpu.VMEM((2 * tq, LANES), F32),
                        pltpu.VMEM((2 * tq, bk), F32)],
        compiler_params=_cparams("parallel", "parallel", "arbitrary"),
        name="sb_attention",
    )(h0, h0, h0)


def _fox_kernel(q_ref, k_ref, v_ref, cq_ref, ck_ref, o_ref, m_ref, acc_ref, z_ref, *, tq):
    i = pl.program_id(2)
    lane = lax.broadcasted_iota(jnp.int32, (1, LANES), 1)
    qs = _stack_heads(q_ref[...], lane, HEAD_DIM ** -0.5 * LOG2E)
    eye = (lax.broadcasted_iota(jnp.int32, (LANES, LANES), 0)
           == lax.broadcasted_iota(jnp.int32, (LANES, LANES), 1))
    cols = []
    for hh in range(2):
        cqr = cq_ref[hh] * LOG2E
        for r in range(tq // LANES):
            seg = jnp.broadcast_to(cqr[:, r * LANES:(r + 1) * LANES], (LANES, LANES))
            cols.append(jnp.sum(jnp.where(eye, seg, 0.0), axis=1, keepdims=True))
    cq = jnp.concatenate(cols, axis=0)
    row = lax.broadcasted_iota(jnp.int32, (tq, tq), 0)
    col = lax.broadcasted_iota(jnp.int32, (tq, tq), 1)
    causal = jnp.concatenate([col <= row, col <= row], axis=0)
    ones = jnp.ones((tq, LANES), BF16)
    m_ref[...] = jnp.full(m_ref.shape, NEG, F32)
    acc_ref[...] = jnp.zeros(acc_ref.shape, F32)

    def scores(kb):
        start = pl.multiple_of(kb * tq, tq)
        k = k_ref[pl.ds(start, tq), :]
        ck = jnp.concatenate([jnp.broadcast_to(ck_ref[0, :, pl.ds(start, tq)] * LOG2E, (tq, tq)),
                              jnp.broadcast_to(ck_ref[1, :, pl.ds(start, tq)] * LOG2E, (tq, tq))], axis=0)
        return (_dot_nt(qs, k) + cq) - ck

    def step(kb, masked, nxt):
        lg = z_ref[...]
        if nxt is not None:
            z_ref[...] = scores(nxt)
        if masked:
            lg = jnp.where(causal, lg, NEG)
        start = pl.multiple_of(kb * tq, tq)
        v = v_ref[pl.ds(start, tq), :]
        m_old = m_ref[...]
        m_new = jnp.maximum(m_old, jnp.max(lg, axis=1, keepdims=True))
        p = jnp.exp2(lg - jnp.concatenate([m_new] * (tq // LANES), axis=1))
        a = jnp.exp2(m_old - m_new)
        acc_ref[...] = (jnp.concatenate([a, a], axis=1) * acc_ref[...]
                        + _dot(p.astype(BF16), jnp.concatenate([v, ones], axis=1)))
        m_ref[...] = m_new

    z_ref[...] = scores(i)

    @pl.when(i == 0)
    def _():
        step(0, True, None)

    @pl.when(i > 0)
    def _():
        step(i, True, i - 1)

        def body(t, _):
            step(i - t, False, jnp.maximum(i - t - 1, 0))
            return 0

        lax.fori_loop(1, i + 1, body, 0)

    out = acc_ref[:, 0:LANES] / acc_ref[:, LANES:2 * LANES]
    o_ref[...] = jnp.where(lane < HEAD_DIM, out[0:tq], out[tq:2 * tq]).astype(o_ref.dtype)


def _fox_attention(h0, cum, batch, seq, tq):
    npair = N_HEADS_FOX // 2
    nq = seq // tq
    base = 3 * (N_HEADS_SB // 2)
    q_off, k_off, v_off = base, base + npair, base + 2 * npair
    cum_k = cum.reshape(batch * N_HEADS_FOX, 1, seq)
    return pl.pallas_call(
        functools.partial(_fox_kernel, tq=tq),
        grid=(batch, npair, nq),
        in_specs=[pl.BlockSpec((tq, LANES), lambda b, p, i: (b * nq + i, q_off + p)),
                  pl.BlockSpec((seq, LANES), lambda b, p, i: (b, k_off + p)),
                  pl.BlockSpec((seq, LANES), lambda b, p, i: (b, v_off + p)),
                  pl.BlockSpec((2, 1, tq), lambda b, p, i: (b * npair + p, 0, i)),
                  pl.BlockSpec((2, 1, seq), lambda b, p, i: (b * npair + p, 0, 0))],
        out_specs=pl.BlockSpec((tq, LANES), lambda b, p, i: (b * nq + i, p)),
        out_shape=jax.ShapeDtypeStruct((batch * seq, N_HEADS_FOX * HEAD_DIM), BF16),
        scratch_shapes=[pltpu.VMEM((2 * tq, LANES), F32),
                        pltpu.VMEM((2 * tq, 2 * LANES), F32),
                        pltpu.VMEM((2 * tq, tq), F32)],
        compiler_params=_cparams("parallel", "parallel", "arbitrary"),
        name="fox_attention",
    )(h0, h0, h0, cum_k, cum_k)


def _rope_q(x, cos_t, sin_t, lane):
    xs = jnp.where(lane < DSA_ROPE // 2, pltpu.roll(x, LANES - DSA_ROPE // 2, 1), pltpu.roll(x, DSA_ROPE // 2, 1))
    return x * cos_t + xs * sin_t


def _rope_idx(x, cos_t, sin_t, lane):
    half = IDX_DIM // 8
    xs = jnp.where((lane & (IDX_DIM - 1)) < half, pltpu.roll(x, LANES - half, 1), pltpu.roll(x, half, 1))
    return x * cos_t + xs * sin_t


def _dsa_prep_kernel(kl_ref, km_ref, c_ref, s_ref, klo_ref, kio_ref):
    lane = lax.broadcasted_iota(jnp.int32, (1, LANES), 1)
    cq, sq, ci, si = _expand_rope_tables(c_ref[...], s_ref[...], lane)
    klo_ref[:, 0:LANES] = _rope_q(kl_ref[:, 0:LANES], cq, sq, lane).astype(BF16)
    klo_ref[:, LANES:2 * LANES] = kl_ref[:, LANES:2 * LANES].astype(BF16)
    yr = _rope_idx(km_ref[...], ci, si, lane)
    kio_ref[...] = jnp.where(lane < IDX_DIM, yr, pltpu.roll(yr, IDX_DIM, 1)).astype(BF16)


def _dsa_prep(hr, tabs, rb):
    m = hr.shape[0]
    kl_blk = IDX_HEADS * IDX_DIM // (2 * LANES)
    km_blk = (IDX_HEADS * IDX_DIM + DSA_KEY_DIM) // LANES
    tspec = pl.BlockSpec((rb, LANES), lambda i: (i, 0))
    return pl.pallas_call(
        _dsa_prep_kernel,
        grid=(m // rb,),
        in_specs=[pl.BlockSpec((rb, 2 * LANES), lambda i: (i, kl_blk)),
                  pl.BlockSpec((rb, LANES), lambda i: (i, km_blk)),
                  tspec, tspec],
        out_specs=[pl.BlockSpec((rb, 2 * LANES), lambda i: (i, 0)),
                   pl.BlockSpec((rb, LANES), lambda i: (i, 0))],
        out_shape=[jax.ShapeDtypeStruct((m, DSA_KEY_DIM), BF16), jax.ShapeDtypeStruct((m, LANES), BF16)],
        compiler_params=_cparams("parallel"),
        name="dsa_key_prep",
    )(hr, hr, *tabs)


def _ordered_to_f32(o):
    return lax.bitcast_convert_type(jnp.where(o >= 0, o, o ^ jnp.int32(0x7FFFFFFF)), F32)


def _f32_to_ordered(x):
    b = lax.bitcast_convert_type(x, jnp.int32)
    return jnp.where(b >= 0, b, b ^ jnp.int32(0x7FFFFFFF))


def _dsa_kernel(q_ref, qi_ref, km_ref, c_ref, s_ref, kl_ref, ki_ref, wuv_ref, o_ref,
                sc_ref, kh_ref, kl_lo_ref, qs_ref, qis_ref, wb_ref, m_ref, acc_ref, z_ref, *, tq, bk, topk, n_grp):
    i = pl.program_id(1)
    nh = N_HEADS_DSA
    lane = lax.broadcasted_iota(jnp.int32, (1, LANES), 1)
    scale = DSA_KEY_DIM ** -0.5 * LOG2E
    gr = nh * tq // n_grp
    rep = bk // LANES

    cq, sq, ci, si = _expand_rope_tables(c_ref[...], s_ref[...], lane)
    for h in range(nh):
        c0 = h * DSA_KEY_DIM
        xr = _rope_q(q_ref[:, c0:c0 + LANES].astype(F32), cq, sq, lane) * scale
        qs_ref[h * tq:(h + 1) * tq, 0:LANES] = xr.astype(BF16)
        qs_ref[h * tq:(h + 1) * tq, LANES:2 * LANES] = (
            q_ref[:, c0 + LANES:c0 + 2 * LANES].astype(F32) * scale).astype(BF16)
    km = km_ref[...]
    for p in range(IDX_HEADS // 2):
        xr = _rope_idx(qi_ref[:, p * LANES:(p + 1) * LANES], ci, si, lane)
        for hh in range(2):
            h = 2 * p + hh
            qis_ref[h * tq:(h + 1) * tq, :] = jnp.where((lane >> 6) == hh, xr, 0.0).astype(BF16)
            wb_ref[h * tq:(h + 1) * tq, :] = jnp.broadcast_to(km[:, IDX_DIM + h:IDX_DIM + h + 1], (tq, LANES))

    nkb = lax.div((i + 1) * tq + bk - 1, bk)
    t_abs = i * tq + lax.broadcasted_iota(jnp.int32, (tq, 1), 0)
    vis_lim = ((t_abs >> 6) + 1) << 6
    col0 = lax.broadcasted_iota(jnp.int32, (tq, bk), 1)

    def score_tile(kb):
        start = pl.multiple_of(kb * bk, bk)
        kd = ki_ref[pl.ds(start, bk), :]
        s = jnp.zeros((tq, bk), F32)
        for g in range(n_grp):
            rows = slice(g * gr, (g + 1) * gr)
            wb = jnp.concatenate([wb_ref[rows, :]] * rep, axis=1)
            lg = jnp.maximum(_dot_nt(qis_ref[rows, :], kd), 0.0) * wb
            for h in range(gr // tq):
                s = s + lg[h * tq:(h + 1) * tq]
        s = s * IDX_SCALE
        s = jnp.where(s == 0.0, 0.0, s)
        s = jnp.where(col0 + start < vis_lim, s, -jnp.inf)
        sc_ref[:, pl.ds(start, bk)] = s

    def score_pair(j, _):
        score_tile(2 * j)
        score_tile(2 * j + 1)
        return 0

    lax.fori_loop(0, nkb >> 1, score_pair, 0)

    @pl.when((nkb & 1) == 1)
    def _():
        score_tile(nkb - 1)

    def keys_body(kb, _):
        start = pl.multiple_of(kb * bk, bk)
        ok = _f32_to_ordered(sc_ref[:, pl.ds(start, bk)])
        kh_ref[:, pl.ds(start, bk)] = (ok >> 16).astype(jnp.int16)
        kl_lo_ref[:, pl.ds(start, bk)] = ((ok & 0xFFFF) - 32768).astype(jnp.int16)
        return 0

    lax.fori_loop(0, nkb, keys_body, 0)

    def count(pred):
        def body(kb, c):
            start = pl.multiple_of(kb * bk, bk)
            hit = jnp.where(pred(sc_ref[:, pl.ds(start, bk)], col0 + start), 1.0, 0.0)
            for j in range(rep):
                c = c + hit[:, j * LANES:(j + 1) * LANES]
            return c
        c = lax.fori_loop(0, nkb, body, jnp.zeros((tq, LANES), F32))
        return jnp.sum(c, axis=1, keepdims=True)

    def wide16(v):
        return jnp.concatenate([jnp.broadcast_to(v, (tq, LANES)).astype(jnp.int16)] * rep, axis=1)

    def count16(ref, pred):
        def body(kb, c):
            start = pl.multiple_of(kb * bk, bk)
            hit = jnp.where(pred(ref[:, pl.ds(start, bk)]), jnp.int16(1), jnp.int16(0))
            for j in range(rep):
                c = c + hit[:, j * LANES:(j + 1) * LANES]
            return c
        c = lax.fori_loop(0, nkb, body, jnp.zeros((tq, LANES), jnp.int16))
        return jnp.sum(c.astype(F32), axis=1, keepdims=True)

    def kth_largest16(ref, need):
        def body(step, lo):
            cand = lo + lax.shift_left(jnp.int32(1), 15 - step)
            cw = wide16(cand)
            cnt = count16(ref, lambda t: t >= cw)
            return jnp.where(cnt >= need, cand, lo)
        return lax.fori_loop(0, 16, body, jnp.full((tq, 1), -32768, jnp.int32))

    t_hi = kth_largest16(kh_ref, topk)
    t_hi_w = wide16(t_hi)
    above = count16(kh_ref, lambda t: t > t_hi_w)

    def bin_body(kb, _):
        start = pl.multiple_of(kb * bk, bk)
        in_bin = kh_ref[:, pl.ds(start, bk)] == t_hi_w
        kl_lo_ref[:, pl.ds(start, bk)] = jnp.where(in_bin, kl_lo_ref[:, pl.ds(start, bk)], jnp.int16(-32768))
        return 0

    lax.fori_loop(0, nkb, bin_body, 0)
    t_lo = kth_largest16(kl_lo_ref, topk - above)
    few = vis_lim <= topk
    o_thr = lax.shift_left(t_hi, 16) | ((t_lo + 32768) & 0xFFFF)
    thr = jnp.where(few, -jnp.inf, _ordered_to_f32(o_thr))
    cnt_gt = count(lambda s, c: s > thr)
    cnt_ge = count(lambda s, c: s >= thr)
    need = topk - cnt_gt
    tie = jnp.logical_and(cnt_ge > topk, jnp.logical_not(few))

    def tie_search(_):
        def jb(step, x):
            cand = x + lax.shift_left(jnp.int32(1), 30 - step)
            cnt = count(lambda s, c: jnp.logical_and(s == thr, c < cand))
            return jnp.where(cnt < need, cand, x)
        return lax.fori_loop(0, 31, jb, jnp.zeros((tq, 1), jnp.int32))

    any_tie = jnp.max(jnp.where(tie, 1.0, 0.0)) > 0.0
    jlim = lax.cond(any_tie, tie_search, lambda _: jnp.zeros((tq, 1), jnp.int32), 0)
    jlim = jnp.where(tie, jlim, jnp.int32(2 ** 30))

    def bias_body(kb, _):
        start = pl.multiple_of(kb * bk, bk)
        s = sc_ref[:, pl.ds(start, bk)]
        c = col0 + start
        keep = jnp.logical_or(s > thr, jnp.logical_and(s == thr, c <= jlim))
        keep = jnp.logical_and(keep, c < vis_lim)
        sc_ref[:, pl.ds(start, bk)] = jnp.where(keep, 0.0, NEG)
        return 0

    lax.fori_loop(0, nkb, bias_body, 0)

    m_ref[...] = jnp.full(m_ref.shape, NEG, F32)
    acc_ref[...] = jnp.zeros(acc_ref.shape, F32)
    lane2 = lax.broadcasted_iota(jnp.int32, (1, DSA_KEY_DIM), 1)

    def qk(kb):
        start = pl.multiple_of(kb * bk, bk)
        kl = kl_ref[pl.ds(start, bk), :]
        for g in range(n_grp):
            rows = slice(g * gr, (g + 1) * gr)
            z_ref[rows, :] = _dot_nt(qs_ref[rows, :], kl)

    def attn_step(kb, prefetch):
        start = pl.multiple_of(kb * bk, bk)
        kl = kl_ref[pl.ds(start, bk), :]
        vl = jnp.where(lane2 < DSA_ROPE, jnp.ones_like(kl), kl)
        bias = sc_ref[:, pl.ds(start, bk)]
        bias_g = jnp.concatenate([bias] * (gr // tq), axis=0)
        if prefetch:
            nstart = pl.multiple_of((kb + 1) * bk, bk)
            kn = kl_ref[pl.ds(nstart, bk), :]
        for g in range(n_grp):
            rows = slice(g * gr, (g + 1) * gr)
            lg = z_ref[rows, :] + bias_g
            if prefetch:
                z_ref[rows, :] = _dot_nt(qs_ref[rows, :], kn)
            m_old = m_ref[rows, :]
            m_new = jnp.maximum(m_old, jnp.max(lg, axis=1, keepdims=True))
            p = jnp.exp2(lg - jnp.concatenate([m_new] * rep, axis=1))
            a = jnp.exp2(m_old - m_new)
            acc_ref[rows, :] = jnp.concatenate([a, a], axis=1) * acc_ref[rows, :] + _dot(p.astype(BF16), vl)
            m_ref[rows, :] = m_new

    qk(0)

    def attn_body(kb, _):
        attn_step(kb, True)
        return 0

    lax.fori_loop(0, nkb - 1, attn_body, 0)
    attn_step(nkb - 1, False)

    for p in range(nh // 2):
        parts = []
        for hh in range(2):
            rows = slice((2 * p + hh) * tq, (2 * p + hh + 1) * tq)
            acc = acc_ref[rows, :]
            inv = 1.0 / jnp.broadcast_to(acc[:, 0:1], acc.shape)
            parts.append((acc * inv).astype(BF16))
        ctx = jnp.concatenate(parts, axis=1)
        o_ref[:, p * LANES:(p + 1) * LANES] = _dot(ctx, wuv_ref[p]).astype(o_ref.dtype)


def _dsa(hq, hr, klr, kid, tabs, wuv2, batch, seq, tq, bk):
    nq = seq // tq
    nh = N_HEADS_DSA
    km_blk = (IDX_HEADS * IDX_DIM + DSA_KEY_DIM) // LANES
    topk = min(IDX_TOPK, seq // 4)
    tspec = pl.BlockSpec((tq, LANES), lambda b, i: (b * nq + i, 0))
    return pl.pallas_call(
        functools.partial(_dsa_kernel, tq=tq, bk=bk, topk=topk, n_grp=4),
        grid=(batch, nq),
        in_specs=[pl.BlockSpec((tq, nh * DSA_KEY_DIM), lambda b, i: (b * nq + i, 0)),
                  pl.BlockSpec((tq, IDX_HEADS * IDX_DIM), lambda b, i: (b * nq + i, 0)),
                  pl.BlockSpec((tq, LANES), lambda b, i: (b * nq + i, km_blk)),
                  tspec, tspec,
                  pl.BlockSpec((seq, DSA_KEY_DIM), lambda b, i: (b, 0)),
                  pl.BlockSpec((seq, LANES), lambda b, i: (b, 0)),
                  pl.BlockSpec(wuv2.shape, lambda b, i: (0, 0, 0))],
        out_specs=pl.BlockSpec((tq, nh * DSA_V_HEAD), lambda b, i: (b * nq + i, 0)),
        out_shape=jax.ShapeDtypeStruct((batch * seq, nh * DSA_V_HEAD), BF16),
        scratch_shapes=[pltpu.VMEM((tq, seq), F32),
                        pltpu.VMEM((tq, seq), jnp.int16),
                        pltpu.VMEM((tq, seq), jnp.int16),
                        pltpu.VMEM((nh * tq, DSA_KEY_DIM), BF16),
                        pltpu.VMEM((nh * tq, LANES), BF16),
                        pltpu.VMEM((nh * tq, LANES), F32),
                        pltpu.VMEM((nh * tq, LANES), F32),
                        pltpu.VMEM((nh * tq, DSA_KEY_DIM), F32),
                        pltpu.VMEM((nh * tq, bk), F32)],
        compiler_params=_cparams("parallel", "arbitrary"),
        name="dsa_attention",
    )(hq, hr, hr, *tabs, klr, kid, wuv2)


def _rope_tables(positions):
    pos = positions.astype(F32).reshape(-1, 1)

    def freqs(half):
        return ROPE_THETA ** (-jnp.arange(half, dtype=F32) / half)

    fq, fi = freqs(DSA_ROPE // 2), freqs(IDX_DIM // 8)
    f_all = jnp.concatenate([fq, fq, fi, fi, jnp.zeros((LANES - DSA_ROPE - IDX_DIM // 4,), F32)])
    ang = pos * f_all[None, :]
    return jnp.cos(ang), jnp.sin(ang)


def _expand_rope_tables(c, s, lane):
    half_q, half_i = DSA_ROPE // 2, IDX_DIM // 8
    cos_q = jnp.where(lane < DSA_ROPE, c, 1.0)
    sin_q = jnp.where(lane < half_q, -s, jnp.where(lane < DSA_ROPE, s, 0.0))
    l64 = lane & (IDX_DIM - 1)
    ci = jnp.where(lane < IDX_DIM, pltpu.roll(c, IDX_DIM, 1), c)
    si = jnp.where(lane < IDX_DIM, pltpu.roll(s, IDX_DIM, 1), s)
    cos_i = jnp.where(l64 < 2 * half_i, ci, 1.0)
    sin_i = jnp.where(l64 < half_i, -si, jnp.where(l64 < 2 * half_i, si, 0.0))
    return cos_q, sin_q, cos_i, sin_i


def _pick(n, *cands):
    for c in cands:
        if n % c == 0:
            return c
    return n


def kernel(x, positions, l0_w_in, l0_b_f, l0_w_o, l0_ln1_g, l0_ln1_b, l0_w_up, l0_conv_w, l0_conv_b, l0_w_down, l0_ln2_g, l0_ln2_b, l1_w_in, l1_w_uv, l1_w_o, l1_ln1_g, l1_ln1_b, l1_w_up, l1_conv_w, l1_conv_b, l1_w_down, l1_ln2_g, l1_ln2_b):
    batch, seq, d = x.shape
    m = batch * seq
    xf = x.reshape(m, d).astype(F32)
    xb = xf.astype(BF16)
    bm = _pick(m, 1024, 512, 256)
    tq_a = _pick(seq, 512, 256, 128)
    bk_sb = _pick(tq_a, 256, 128)

    n_qkv = 3 * (N_HEADS_SB + N_HEADS_FOX) * HEAD_DIM
    h0 = _matmul(xb, l0_w_in[:, :n_qkv].astype(BF16), BF16, bm, 1536, "l0_in_proj")
    w_f = jnp.pad(l0_w_in[:, n_qkv:], ((0, 0), (0, LANES - N_HEADS_FOX))).astype(BF16)
    fl = _matmul(xb, w_f, F32, bm, LANES, "l0_forget_proj")
    cum = _forget_cumsum(fl, l0_b_f, batch, seq)
    oa = _sb_attention(h0, batch, seq, tq_a, bk_sb)
    ob = _fox_attention(h0, cum, batch, seq, tq_a)
    wo = l0_w_o.astype(BF16)
    na = N_HEADS_SB * HEAD_DIM
    x1f, x1b = _matmul_ln([(oa, wo[:na]), (ob, wo[na:])], xf, l0_ln1_g, l0_ln1_b, 1024, "l0_out_proj_ln")
    a0 = _ffn_up(x1b, l0_w_up.astype(BF16), l0_conv_w, l0_conv_b, seq, FFN_BM, FFN_BN, "l0_ffn_up_act")
    x2f, x2b = _matmul_ln([(a0, l0_w_down.astype(BF16))], x1f, l0_ln2_g, l0_ln2_b, 512, "l0_ffn_down_ln")

    c0 = N_HEADS_DSA * DSA_KEY_DIM
    c2 = c0 + DSA_KEY_DIM
    c3 = c2 + IDX_HEADS * IDX_DIM
    n_in = l1_w_in.shape[1]
    hq = _matmul(x2b, l1_w_in[:, :c0].astype(BF16), BF16, bm, 1024, "l1_q_proj")
    w_rest = jnp.concatenate([l1_w_in[:, c2:c3], l1_w_in[:, c0:c2], l1_w_in[:, c3:],
                              jnp.zeros((d, LANES - (n_in - c3)), F32)], axis=1).astype(BF16)
    hr = _matmul(x2b, w_rest, F32, 512, w_rest.shape[1], "l1_kidx_proj")
    tabs = _rope_tables(positions)
    klr, kid = _dsa_prep(hr, tabs, 512)
    wuv = l1_w_uv.astype(BF16)
    zpad = jnp.zeros((DSA_ROPE, DSA_V_HEAD), BF16)
    zblk = jnp.zeros((DSA_KEY_DIM, DSA_V_HEAD), BF16)
    wuv2 = jnp.stack([
        jnp.concatenate([jnp.concatenate([zpad, wuv[2 * p], zblk], axis=0),
                         jnp.concatenate([zblk, zpad, wuv[2 * p + 1]], axis=0)], axis=1)
        for p in range(N_HEADS_DSA // 2)])
    o1 = _dsa(hq, hr, klr, kid, tabs, wuv2, batch, seq, 128, _pick(seq, 512, 256))
    x3f, x3b = _matmul_ln([(o1, l1_w_o.astype(BF16))], x2f, l1_ln1_g, l1_ln1_b, 1024, "l1_out_proj_ln")
    a1 = _ffn_up(x3b, l1_w_up.astype(BF16), l1_conv_w, l1_conv_b, seq, FFN_BM, FFN_BN, "l1_ffn_up_act")
    x4f, _ = _matmul_ln([(a1, l1_w_down.astype(BF16))], x3f, l1_ln2_g, l1_ln2_b, 512, "l1_ffn_down_ln")
    return x4f.reshape(batch, seq, d)
```

```python
import functools
from typing import NamedTuple

import jax
import jax.numpy as jnp
from jax import lax
from jax.experimental import pallas as pl
from jax.experimental.pallas import tpu as pltpu

F32 = jnp.float32
BF16 = jnp.bfloat16

LANES = 128
VMEM_LIMIT = 56 * 1024 * 1024

HEAD_DIM = 64
N_HEADS_SB = 8
N_HEADS_FOX = 8
N_HEADS_DSA = 16
DSA_KEY_DIM = 256
DSA_ROPE = 64
DSA_V_HEAD = 64
IDX_HEADS = 16
IDX_DIM = 64
IDX_TOPK = 256
IDX_SCALE = (IDX_HEADS * IDX_DIM) ** -0.5
CHUNK = 64
ROPE_THETA = 500000.0
LN_EPS = 1e-5
DEPTH = 2
ALPHA = (2 * DEPTH) ** 0.25
NEG = -1e30
INT_MIN = -(2 ** 31)
LOG2E = 1.4426950408889634
HEAD_SHIFT = HEAD_DIM.bit_length() - 1
CHUNK_SHIFT = CHUNK.bit_length() - 1


class _Tiles(NamedTuple):
    mm_rows: int
    in_cols: int
    q_cols: int
    aux_rows: int
    ln_rows_attn: int
    ln_rows_ffn: int
    ffn_rows: int
    ffn_cols: int
    attn_q: int
    sb_keys: int
    dsa_q: int
    dsa_keys: int


def _pick(n, *cands):
    for c in cands:
        if n % c == 0:
            return c
    return n


def _plan(batch, seq, d_ff):
    m = batch * seq
    attn_q = _pick(seq, 512, 256, 128)
    return _Tiles(mm_rows=_pick(m, 1024, 512, 256), in_cols=1536, q_cols=1024, aux_rows=_pick(m, 512, 256),
                  ln_rows_attn=_pick(m, 1024, 512, 256), ln_rows_ffn=_pick(m, 512, 256),
                  ffn_rows=_pick(m, 512, 256), ffn_cols=_pick(d_ff, 1408, 256, 128),
                  attn_q=attn_q, sb_keys=attn_q // 2, dsa_q=128, dsa_keys=_pick(seq, 512, 256))


def _cparams(*sem):
    return pltpu.CompilerParams(dimension_semantics=sem, vmem_limit_bytes=VMEM_LIMIT)


def _dot(a, b):
    return jnp.dot(a, b, preferred_element_type=F32)


def _dot_nt(a, b):
    return lax.dot_general(a, b, (((1,), (1,)), ((), ())), preferred_element_type=F32)


def _split3(x):
    x1 = x.astype(BF16)
    r1 = x - x1.astype(F32)
    x2 = r1.astype(BF16)
    x3 = (r1 - x2.astype(F32)).astype(BF16)
    return x1, x2, x3


def _mm_kernel(x_ref, w_ref, o_ref):
    o_ref[...] = _dot(x_ref[...], w_ref[...]).astype(o_ref.dtype)


def _matmul(x, w, out_dtype, bm, bn, name):
    m, k = x.shape
    n = w.shape[1]
    return pl.pallas_call(
        _mm_kernel,
        grid=(m // bm, n // bn),
        in_specs=[pl.BlockSpec((bm, k), lambda i, j: (i, 0)),
                  pl.BlockSpec((k, bn), lambda i, j: (0, j))],
        out_specs=pl.BlockSpec((bm, bn), lambda i, j: (i, j)),
        out_shape=jax.ShapeDtypeStruct((m, n), out_dtype),
        compiler_params=_cparams("parallel", "parallel"),
        name=name,
    )(x, w)


def _layer_norm_rows(y, g, b):
    mu = jnp.mean(y, axis=-1, keepdims=True)
    yc = y - mu
    var = jnp.mean(yc * yc, axis=-1, keepdims=True)
    return yc * lax.rsqrt(var + LN_EPS) * g + b


def _mm_ln_kernel(*refs, n_pairs):
    xs = refs[:n_pairs]
    ws = refs[n_pairs:2 * n_pairs]
    res_ref, g_ref, b_ref, of_ref, ob_ref = refs[2 * n_pairs:]
    acc = _dot(xs[0][...], ws[0][...])
    for p in range(1, n_pairs):
        acc = acc + _dot(xs[p][...], ws[p][...])
    out = _layer_norm_rows(ALPHA * res_ref[...] + acc, g_ref[...], b_ref[...])
    of_ref[...] = out
    ob_ref[...] = out.astype(BF16)


def _matmul_ln(pairs, res, g, b, bm, name):
    m, d = res.shape
    n_pairs = len(pairs)
    in_specs = ([pl.BlockSpec((bm, x.shape[1]), lambda i: (i, 0)) for x, _ in pairs]
                + [pl.BlockSpec(w.shape, lambda i: (0, 0)) for _, w in pairs]
                + [pl.BlockSpec((bm, d), lambda i: (i, 0)),
                   pl.BlockSpec((1, d), lambda i: (0, 0)),
                   pl.BlockSpec((1, d), lambda i: (0, 0))])
    return pl.pallas_call(
        functools.partial(_mm_ln_kernel, n_pairs=n_pairs),
        grid=(m // bm,),
        in_specs=in_specs,
        out_specs=[pl.BlockSpec((bm, d), lambda i: (i, 0)),
                   pl.BlockSpec((bm, d), lambda i: (i, 0))],
        out_shape=[jax.ShapeDtypeStruct((m, d), F32), jax.ShapeDtypeStruct((m, d), BF16)],
        compiler_params=_cparams("parallel"),
        name=name,
    )(*[x for x, _ in pairs], *[w for _, w in pairs], res, g.reshape(1, d), b.reshape(1, d))


HALO = 16


def _ffn_up_kernel(x_ref, xh_ref, wg_ref, wu_ref, cw_ref, cb_ref, o_ref, *, bm, seq):
    i = pl.program_id(0)
    starts_seq = lax.rem(i * bm, seq) == 0
    xh = xh_ref[...]
    xh = jnp.where(starts_seq, jnp.zeros_like(xh), xh)
    x = x_ref[...]
    gate = _dot(jnp.concatenate([xh, x], axis=0), wg_ref[...])
    up = _dot(x, wu_ref[...])
    g = gate[HALO:, :]
    g1 = pltpu.roll(gate, 1, 0)[HALO:, :]
    g2 = pltpu.roll(gate, 2, 0)[HALO:, :]
    cw = cw_ref[...]
    conv = cb_ref[...] + cw[0:1, :] * g2
    conv = conv + cw[1:2, :] * g1
    conv = conv + cw[2:3, :] * g
    cdf = 0.5 * (1.0 + jnp.tanh(0.7978845608028654 * (conv + 0.044715 * (conv * conv * conv))))
    o_ref[...] = ((conv * cdf) * up).astype(o_ref.dtype)


def _ffn_up(x, w_up, conv_w, conv_b, seq, bm, bn, name):
    m, d = x.shape
    f = w_up.shape[1] // 2
    nb = f // bn
    hb = bm // HALO
    return pl.pallas_call(
        functools.partial(_ffn_up_kernel, bm=bm, seq=seq),
        grid=(m // bm, nb),
        in_specs=[pl.BlockSpec((bm, d), lambda i, j: (i, 0)),
                  pl.BlockSpec((HALO, d), lambda i, j: (jnp.maximum(i * hb - 1, 0), 0)),
                  pl.BlockSpec((d, bn), lambda i, j: (0, j)),
                  pl.BlockSpec((d, bn), lambda i, j: (0, nb + j)),
                  pl.BlockSpec((3, bn), lambda i, j: (0, j)),
                  pl.BlockSpec((1, bn), lambda i, j: (0, j))],
        out_specs=pl.BlockSpec((bm, bn), lambda i, j: (i, j)),
        out_shape=jax.ShapeDtypeStruct((m, f), BF16),
        compiler_params=_cparams("parallel", "parallel"),
        name=name,
    )(x, x, w_up, w_up, conv_w, conv_b.reshape(1, f))


def _cum_kernel(f_ref, bias_ref, o_ref, *, rows_per_seq):
    x = f_ref[...] + bias_ref[...]
    ls = jnp.minimum(x, 0.0) - jnp.log(1.0 + jnp.exp(-jnp.abs(x)))
    r = x.shape[0]
    incl = (lax.broadcasted_iota(jnp.int32, (LANES, LANES), 0)
            <= lax.broadcasted_iota(jnp.int32, (LANES, LANES), 1)).astype(BF16)
    a1, a2, a3 = _split3(ls)
    cs = _dot(a1, incl) + _dot(a2, incl) + _dot(a3, incl)
    tot = jnp.broadcast_to(cs[:, LANES - 1:LANES], (r, LANES))
    ri = lax.broadcasted_iota(jnp.int32, (r, r), 0)
    ci = lax.broadcasted_iota(jnp.int32, (r, r), 1)
    shift = rows_per_seq.bit_length() - 1
    same_seq = (ri >> shift) == (ci >> shift)
    before = jnp.where(same_seq & (ci < ri), 1.0, 0.0).astype(BF16)
    t1, t2, t3 = _split3(tot)
    off = _dot(before, t1) + _dot(before, t2) + _dot(before, t3)
    o_ref[...] = cs + off


def _forget_cumsum(fl, b_f, batch, seq):
    nh = N_HEADS_FOX
    f = fl[:, :nh].reshape(batch, seq, nh).transpose(0, 2, 1).reshape(batch * nh * (seq // LANES), LANES)
    bias = jnp.broadcast_to(jnp.tile(b_f.astype(F32), batch)[:, None, None],
                            (batch * nh, seq // LANES, LANES)).reshape(f.shape)
    cum = pl.pallas_call(
        functools.partial(_cum_kernel, rows_per_seq=seq // LANES),
        out_shape=jax.ShapeDtypeStruct(f.shape, F32),
        compiler_params=pltpu.CompilerParams(vmem_limit_bytes=VMEM_LIMIT),
        name="forget_cumsum",
    )(f, bias)
    return cum.reshape(batch * nh, seq)


def _neg_abs(x):
    return lax.bitcast_convert_type(lax.bitcast_convert_type(x, jnp.int32) | jnp.int32(INT_MIN), F32)


def _stack_heads(q, lane, scale):
    qf = q.astype(F32) * scale
    return jnp.concatenate([jnp.where(lane < HEAD_DIM, qf, 0.0), jnp.where(lane >= HEAD_DIM, qf, 0.0)],
                           axis=0).astype(BF16)


def _sb_kernel(q_ref, k_ref, v_ref, o_ref, acc_ref, c_ref, z_ref, *, tq, bk):
    i = pl.program_id(2)
    lane = lax.broadcasted_iota(jnp.int32, (1, LANES), 1)
    qs = _stack_heads(q_ref[...], lane, HEAD_DIM ** -0.5 * LOG2E)
    rj = lax.broadcasted_iota(jnp.int32, (2 * bk, bk), 0)
    cs_ = lax.broadcasted_iota(jnp.int32, (2 * bk, bk), 1)
    neg_suffix = jnp.where((rj & (bk - 1)) >= cs_, -1.0, 0.0).astype(BF16)
    row = lax.broadcasted_iota(jnp.int32, (tq, bk), 0) + i * tq
    col = lax.broadcasted_iota(jnp.int32, (tq, bk), 1)
    acc_ref[...] = jnp.zeros(acc_ref.shape, F32)
    c_ref[...] = jnp.zeros(c_ref.shape, F32)

    def scores(kb):
        start = pl.multiple_of(kb * bk, bk)
        return _dot_nt(qs, k_ref[pl.ds(start, bk), :])

    def step(kb, masked, nxt):
        z = z_ref[...]
        if nxt is not None:
            z_ref[...] = scores(nxt)
        start = pl.multiple_of(kb * bk, bk)
        v = v_ref[pl.ds(start, bk), :]
        sp = jnp.maximum(z, 0.0) + jnp.log(1.0 + jnp.exp2(_neg_abs(z))) * LOG2E
        if masked:
            keep = (col + start) < row
            keep = jnp.concatenate([keep, keep], axis=0)
            sp = jnp.where(keep, sp, 0.0)
        hi = sp.astype(BF16)
        lo = (sp - hi.astype(F32)).astype(BF16)
        cs = _dot(jnp.concatenate([hi, lo], axis=1), neg_suffix)
        c = c_ref[...]
        w = jnp.exp2(z + cs + jnp.concatenate([c] * (bk // LANES), axis=1))
        if masked:
            w = jnp.where(keep, w, 0.0)
        acc_ref[...] += _dot(w.astype(BF16), v)
        c_ref[...] = c + jnp.broadcast_to(cs[:, 0:1], c.shape)

    per = tq // bk
    top = i * per + per - 1
    z_ref[...] = scores(top)
    for r in range(per - 1):
        step(top - r, True, top - r - 1)

    @pl.when(i == 0)
    def _():
        step(0, True, None)

    @pl.when(i > 0)
    def _():
        step(i * per, True, i * per - 1)

        def body(u, _):
            kb = i * per - 1 - 2 * u
            step(kb, False, kb - 1)
            step(kb - 1, False, jnp.maximum(kb - 2, 0))
            return 0

        lax.fori_loop(0, i * (per // 2), body, 0)

    o_ref[...] = jnp.where(lane < HEAD_DIM, acc_ref[0:tq, :], acc_ref[tq:2 * tq, :]).astype(o_ref.dtype)


def _sb_attention(h0, batch, seq, tq, bk):
    assert tq == 2 * bk and seq % tq == 0, (seq, tq, bk)
    npair = N_HEADS_SB // 2
    nq = seq // tq
    q_off, k_off, v_off = 0, npair, 2 * npair
    return pl.pallas_call(
        functools.partial(_sb_kernel, tq=tq, bk=bk),
        grid=(batch, npair, nq),
        in_specs=[pl.BlockSpec((tq, LANES), lambda b, p, i: (b * nq + i, q_off + p)),
                  pl.BlockSpec((seq, LANES), lambda b, p, i: (b, k_off + p)),
                  pl.BlockSpec((seq, LANES), lambda b, p, i: (b, v_off + p))],
        out_specs=pl.BlockSpec((tq, LANES), lambda b, p, i: (b * nq + i, p)),
        out_shape=jax.ShapeDtypeStruct((batch * seq, N_HEADS_SB * HEAD_DIM), BF16),
        scratch_shapes=[pltpu.VMEM((2 * tq, LANES), F32),
                        pltpu.VMEM((2 * tq, LANES), F32),
                        pltpu.VMEM((2 * tq, bk), F32)],
        compiler_params=_cparams("parallel", "parallel", "arbitrary"),
        name="sb_attention",
    )(h0, h0, h0)


def _fox_kernel(q_ref, k_ref, v_ref, cq_ref, ck_ref, o_ref, m_ref, acc_ref, z_ref, *, tq):
    i = pl.program_id(2)
    lane = lax.broadcasted_iota(jnp.int32, (1, LANES), 1)
    qs = _stack_heads(q_ref[...], lane, HEAD_DIM ** -0.5 * LOG2E)
    eye = (lax.broadcasted_iota(jnp.int32, (LANES, LANES), 0)
           == lax.broadcasted_iota(jnp.int32, (LANES, LANES), 1))
    cols = []
    for hh in range(2):
        cqr = cq_ref[hh] * LOG2E
        for r in range(tq // LANES):
            seg = jnp.broadcast_to(cqr[:, r * LANES:(r + 1) * LANES], (LANES, LANES))
            cols.append(jnp.sum(jnp.where(eye, seg, 0.0), axis=1, keepdims=True))
    cq = jnp.concatenate(cols, axis=0)
    row = lax.broadcasted_iota(jnp.int32, (tq, tq), 0)
    col = lax.broadcasted_iota(jnp.int32, (tq, tq), 1)
    causal = jnp.concatenate([col <= row, col <= row], axis=0)
    ones = jnp.ones((tq, LANES), BF16)
    m_ref[...] = jnp.full(m_ref.shape, NEG, F32)
    acc_ref[...] = jnp.zeros(acc_ref.shape, F32)

    def scores(kb):
        start = pl.multiple_of(kb * tq, tq)
        k = k_ref[pl.ds(start, tq), :]
        ck = jnp.concatenate([jnp.broadcast_to(ck_ref[0, :, pl.ds(start, tq)] * LOG2E, (tq, tq)),
                              jnp.broadcast_to(ck_ref[1, :, pl.ds(start, tq)] * LOG2E, (tq, tq))], axis=0)
        return (_dot_nt(qs, k) + cq) - ck

    def step(kb, masked, nxt):
        lg = z_ref[...]
        if nxt is not None:
            z_ref[...] = scores(nxt)
        if masked:
            lg = jnp.where(causal, lg, NEG)
        start = pl.multiple_of(kb * tq, tq)
        v = v_ref[pl.ds(start, tq), :]
        m_old = m_ref[...]
        m_new = jnp.maximum(m_old, jnp.max(lg, axis=1, keepdims=True))
        p = jnp.exp2(lg - jnp.concatenate([m_new] * (tq // LANES), axis=1))
        a = jnp.exp2(m_old - m_new)
        acc_ref[...] = (jnp.concatenate([a, a], axis=1) * acc_ref[...]
                        + _dot(p.astype(BF16), jnp.concatenate([v, ones], axis=1)))
        m_ref[...] = m_new

    z_ref[...] = scores(i)

    @pl.when(i == 0)
    def _():
        step(0, True, None)

    @pl.when(i > 0)
    def _():
        step(i, True, i - 1)

        def body(t, _):
            step(i - t, False, jnp.maximum(i - t - 1, 0))
            return 0

        lax.fori_loop(1, i + 1, body, 0)

    out = acc_ref[:, 0:LANES] / acc_ref[:, LANES:2 * LANES]
    o_ref[...] = jnp.where(lane < HEAD_DIM, out[0:tq], out[tq:2 * tq]).astype(o_ref.dtype)


def _fox_attention(h0, cum, batch, seq, tq):
    npair = N_HEADS_FOX // 2
    nq = seq // tq
    base = 3 * (N_HEADS_SB // 2)
    q_off, k_off, v_off = base, base + npair, base + 2 * npair
    cum_k = cum.reshape(batch * N_HEADS_FOX, 1, seq)
    return pl.pallas_call(
        functools.partial(_fox_kernel, tq=tq),
        grid=(batch, npair, nq),
        in_specs=[pl.BlockSpec((tq, LANES), lambda b, p, i: (b * nq + i, q_off + p)),
                  pl.BlockSpec((seq, LANES), lambda b, p, i: (b, k_off + p)),
                  pl.BlockSpec((seq, LANES), lambda b, p, i: (b, v_off + p)),
                  pl.BlockSpec((2, 1, tq), lambda b, p, i: (b * npair + p, 0, i)),
                  pl.BlockSpec((2, 1, seq), lambda b, p, i: (b * npair + p, 0, 0))],
        out_specs=pl.BlockSpec((tq, LANES), lambda b, p, i: (b * nq + i, p)),
        out_shape=jax.ShapeDtypeStruct((batch * seq, N_HEADS_FOX * HEAD_DIM), BF16),
        scratch_shapes=[pltpu.VMEM((2 * tq, LANES), F32),
                        pltpu.VMEM((2 * tq, 2 * LANES), F32),
                        pltpu.VMEM((2 * tq, tq), F32)],
        compiler_params=_cparams("parallel", "parallel", "arbitrary"),
        name="fox_attention",
    )(h0, h0, h0, cum_k, cum_k)


def _rope_q(x, cos_t, sin_t, lane):
    xs = jnp.where(lane < DSA_ROPE // 2, pltpu.roll(x, LANES - DSA_ROPE // 2, 1), pltpu.roll(x, DSA_ROPE // 2, 1))
    return x * cos_t + xs * sin_t


def _rope_idx(x, cos_t, sin_t, lane):
    half = IDX_DIM // 8
    xs = jnp.where((lane & (IDX_DIM - 1)) < half, pltpu.roll(x, LANES - half, 1), pltpu.roll(x, half, 1))
    return x * cos_t + xs * sin_t


def _dsa_prep_kernel(kl_ref, km_ref, c_ref, s_ref, klo_ref, kio_ref):
    lane = lax.broadcasted_iota(jnp.int32, (1, LANES), 1)
    cq, sq, ci, si = _expand_rope_tables(c_ref[...], s_ref[...], lane)
    klo_ref[:, 0:LANES] = _rope_q(kl_ref[:, 0:LANES], cq, sq, lane).astype(BF16)
    klo_ref[:, LANES:2 * LANES] = kl_ref[:, LANES:2 * LANES].astype(BF16)
    yr = _rope_idx(km_ref[...], ci, si, lane)
    kio_ref[...] = jnp.where(lane < IDX_DIM, yr, pltpu.roll(yr, IDX_DIM, 1)).astype(BF16)


def _dsa_prep(hr, tabs, rb):
    m = hr.shape[0]
    kl_blk = IDX_HEADS * IDX_DIM // (2 * LANES)
    km_blk = (IDX_HEADS * IDX_DIM + DSA_KEY_DIM) // LANES
    tspec = pl.BlockSpec((rb, LANES), lambda i: (i, 0))
    return pl.pallas_call(
        _dsa_prep_kernel,
        grid=(m // rb,),
        in_specs=[pl.BlockSpec((rb, 2 * LANES), lambda i: (i, kl_blk)),
                  pl.BlockSpec((rb, LANES), lambda i: (i, km_blk)),
                  tspec, tspec],
        out_specs=[pl.BlockSpec((rb, 2 * LANES), lambda i: (i, 0)),
                   pl.BlockSpec((rb, LANES), lambda i: (i, 0))],
        out_shape=[jax.ShapeDtypeStruct((m, DSA_KEY_DIM), BF16), jax.ShapeDtypeStruct((m, LANES), BF16)],
        compiler_params=_cparams("parallel"),
        name="dsa_key_prep",
    )(hr, hr, *tabs)


def _ordered_to_f32(o):
    return lax.bitcast_convert_type(jnp.where(o >= 0, o, o ^ jnp.int32(0x7FFFFFFF)), F32)


def _f32_to_ordered(x):
    b = lax.bitcast_convert_type(x, jnp.int32)
    return jnp.where(b >= 0, b, b ^ jnp.int32(0x7FFFFFFF))


def _dsa_kernel(q_ref, qi_ref, km_ref, c_ref, s_ref, kl_ref, ki_ref, wuv_ref, o_ref,
                sc_ref, sct_ref, kh_ref, kl_lo_ref, qs_ref, qis_ref, m_ref, acc_ref, z_ref, *, tq, bk, topk, n_grp):
    i = pl.program_id(1)
    nh = N_HEADS_DSA
    lane = lax.broadcasted_iota(jnp.int32, (1, LANES), 1)
    scale = DSA_KEY_DIM ** -0.5 * LOG2E
    gr = nh * tq // n_grp
    rep = bk // LANES

    cq, sq, ci, si = _expand_rope_tables(c_ref[...], s_ref[...], lane)
    for h in range(nh):
        c0 = h * DSA_KEY_DIM
        xr = _rope_q(q_ref[:, c0:c0 + LANES].astype(F32), cq, sq, lane) * scale
        qs_ref[h * tq:(h + 1) * tq, 0:LANES] = xr.astype(BF16)
        qs_ref[h * tq:(h + 1) * tq, LANES:2 * LANES] = (
            q_ref[:, c0 + LANES:c0 + 2 * LANES].astype(F32) * scale).astype(BF16)
    w_rows = km_ref[...].T
    for p in range(IDX_HEADS // 2):
        xr = _rope_idx(qi_ref[:, p * LANES:(p + 1) * LANES], ci, si, lane)
        for hh in range(2):
            h = 2 * p + hh
            qis_ref[h * tq:(h + 1) * tq, :] = jnp.where((lane >> HEAD_SHIFT) == hh, xr, 0.0).astype(BF16)

    nkb = lax.div((i + 1) * tq + bk - 1, bk)
    t_abs = i * tq + lane
    vis_lim = ((t_abs >> CHUNK_SHIFT) + 1) << CHUNK_SHIFT
    key0 = lax.broadcasted_iota(jnp.int32, (bk, 1), 0)
    hpg = nh // n_grp

    def score_tile(kb):
        start = pl.multiple_of(kb * bk, bk)
        kd = ki_ref[pl.ds(start, bk), :]
        s = jnp.zeros((bk, tq), F32)
        for g in range(n_grp):
            lg = jnp.maximum(_dot_nt(kd, qis_ref[g * gr:(g + 1) * gr, :]), 0.0)
            for hh in range(hpg):
                h = g * hpg + hh
                s = s + lg[:, hh * tq:(hh + 1) * tq] * w_rows[IDX_DIM + h:IDX_DIM + h + 1, :]
        s = s * IDX_SCALE
        s = jnp.where(s == 0.0, 0.0, s)
        s = jnp.where(key0 + start < vis_lim, s, -jnp.inf)
        sct_ref[pl.ds(start, bk), :] = s

    def score_pair(j, _):
        score_tile(2 * j)
        score_tile(2 * j + 1)
        return 0

    lax.fori_loop(0, nkb >> 1, score_pair, 0)

    @pl.when((nkb & 1) == 1)
    def _():
        score_tile(nkb - 1)

    def keys_body(kb, _):
        start = pl.multiple_of(kb * bk, bk)
        ok = _f32_to_ordered(sct_ref[pl.ds(start, bk), :])
        kh_ref[pl.ds(start, bk), :] = (ok >> 16).astype(jnp.int16)
        kl_lo_ref[pl.ds(start, bk), :] = ((ok & 0xFFFF) - 32768).astype(jnp.int16)
        return 0

    lax.fori_loop(0, nkb, keys_body, 0)
    pk = 16

    def tree_sum(parts):
        while len(parts) > 1:
            parts = [parts[j] + parts[j + 1] for j in range(0, len(parts) - 1, 2)] + ([parts[-1]] if len(parts) % 2 else [])
        return parts[0]

    def count(pred):
        def body(kb, c):
            start = pl.multiple_of(kb * bk, bk)
            hit = jnp.where(pred(sct_ref[pl.ds(start, bk), :], key0 + start), 1.0, 0.0)
            return c + tree_sum([hit[j * 8:(j + 1) * 8, :] for j in range(bk // 8)])
        c = lax.fori_loop(0, nkb, body, jnp.zeros((8, tq), F32))
        return jnp.sum(c, axis=0, keepdims=True)

    def wide16(v):
        return jnp.broadcast_to(v, (pk, tq)).astype(jnp.int16)

    def count16(ref, pred):
        def body(kb, c):
            start = pl.multiple_of(kb * bk, bk)
            t = ref[pl.ds(start, bk), :]
            hits = [jnp.where(pred(t[j * pk:(j + 1) * pk, :]), jnp.int16(1), jnp.int16(0)) for j in range(bk // pk)]
            return c + tree_sum(hits)
        c = lax.fori_loop(0, nkb, body, jnp.zeros((pk, tq), jnp.int16))
        return jnp.sum(c.astype(F32), axis=0, keepdims=True)

    def kth_largest16(ref, need):
        def body(step, lo):
            cand = lo + lax.shift_left(jnp.int32(1), 15 - step)
            cw = wide16(cand)
            cnt = count16(ref, lambda t: t >= cw)
            return jnp.where(cnt >= need, cand, lo)
        return lax.fori_loop(0, 16, body, jnp.full((1, tq), -32768, jnp.int32))

    t_hi = kth_largest16(kh_ref, topk)
    t_hi_w = wide16(t_hi)
    above = count16(kh_ref, lambda t: t > t_hi_w)
    t_hi_tile = jnp.concatenate([t_hi_w] * (bk // pk), axis=0)

    def bin_body(kb, _):
        start = pl.multiple_of(kb * bk, bk)
        in_bin = kh_ref[pl.ds(start, bk), :] == t_hi_tile
        kl_lo_ref[pl.ds(start, bk), :] = jnp.where(in_bin, kl_lo_ref[pl.ds(start, bk), :], jnp.int16(-32768))
        return 0

    lax.fori_loop(0, nkb, bin_body, 0)
    t_lo = kth_largest16(kl_lo_ref, topk - above)
    few = vis_lim <= topk
    o_thr = lax.shift_left(t_hi, 16) | ((t_lo + 32768) & 0xFFFF)
    thr = jnp.where(few, -jnp.inf, _ordered_to_f32(o_thr))
    cnt_gt = count(lambda s, c: s > thr)
    cnt_ge = count(lambda s, c: s >= thr)
    need = topk - cnt_gt
    tie = jnp.logical_and(cnt_ge > topk, jnp.logical_not(few))

    def tie_search(_):
        def jb(step, x):
            cand = x + lax.shift_left(jnp.int32(1), 30 - step)
            cnt = count(lambda s, c: jnp.logical_and(s == thr, c < cand))
            return jnp.where(cnt < need, cand, x)
        return lax.fori_loop(0, 31, jb, jnp.zeros((1, tq), jnp.int32))

    any_tie = jnp.max(jnp.where(tie, 1.0, 0.0)) > 0.0
    jlim = lax.cond(any_tie, tie_search, lambda _: jnp.zeros((1, tq), jnp.int32), 0)
    jlim = jnp.where(tie, jlim, jnp.int32(2 ** 30))

    def bias_body(kb, _):
        start = pl.multiple_of(kb * bk, bk)
        s = sct_ref[pl.ds(start, bk), :]
        c = key0 + start
        keep = jnp.logical_or(s > thr, jnp.logical_and(s == thr, c <= jlim))
        keep = jnp.logical_and(keep, c < vis_lim)
        sc_ref[:, pl.ds(start, bk)] = jnp.where(keep, 0.0, NEG).astype(BF16).T
        return 0

    lax.fori_loop(0, nkb, bias_body, 0)

    m_ref[...] = jnp.full(m_ref.shape, NEG, F32)
    acc_ref[...] = jnp.zeros(acc_ref.shape, F32)
    lane2 = lax.broadcasted_iota(jnp.int32, (1, DSA_KEY_DIM), 1)

    def qk(kb):
        start = pl.multiple_of(kb * bk, bk)
        kl = kl_ref[pl.ds(start, bk), :]
        for g in range(n_grp):
            rows = slice(g * gr, (g + 1) * gr)
            z_ref[rows, :] = _dot_nt(qs_ref[rows, :], kl)

    def attn_step(kb, prefetch):
        start = pl.multiple_of(kb * bk, bk)
        kl = kl_ref[pl.ds(start, bk), :]
        vl = jnp.where(lane2 < DSA_ROPE, jnp.ones_like(kl), kl)
        bias = sc_ref[:, pl.ds(start, bk)].astype(F32)
        bias_g = jnp.concatenate([bias] * (gr // tq), axis=0)
        if prefetch:
            nstart = pl.multiple_of((kb + 1) * bk, bk)
            kn = kl_ref[pl.ds(nstart, bk), :]
        for g in range(n_grp):
            rows = slice(g * gr, (g + 1) * gr)
            lg = z_ref[rows, :] + bias_g
            if prefetch:
                z_ref[rows, :] = _dot_nt(qs_ref[rows, :], kn)
            m_old = m_ref[rows, :]
            m_new = jnp.maximum(m_old, jnp.max(lg, axis=1, keepdims=True))
            p = jnp.exp2(lg - jnp.concatenate([m_new] * rep, axis=1))
            a = jnp.exp2(m_old - m_new)
            acc_ref[rows, :] = jnp.concatenate([a, a], axis=1) * acc_ref[rows, :] + _dot(p.astype(BF16), vl)
            m_ref[rows, :] = m_new

    qk(0)

    def attn_body(kb, _):
        attn_step(kb, True)
        return 0

    lax.fori_loop(0, nkb - 1, attn_body, 0)
    attn_step(nkb - 1, False)

    for p in range(nh // 2):
        parts = []
        for hh in range(2):
            rows = slice((2 * p + hh) * tq, (2 * p + hh + 1) * tq)
            acc = acc_ref[rows, :]
            inv = 1.0 / jnp.broadcast_to(acc[:, 0:1], acc.shape)
            parts.append((acc * inv).astype(BF16))
        ctx = jnp.concatenate(parts, axis=1)
        o_ref[:, p * LANES:(p + 1) * LANES] = _dot(ctx, wuv_ref[p]).astype(o_ref.dtype)


def _dsa(hq, hr, klr, kid, tabs, wuv2, batch, seq, tq, bk):
    nq = seq // tq
    nh = N_HEADS_DSA
    km_blk = (IDX_HEADS * IDX_DIM + DSA_KEY_DIM) // LANES
    topk = min(IDX_TOPK, seq // 4)
    tspec = pl.BlockSpec((tq, LANES), lambda b, i: (b * nq + i, 0))
    return pl.pallas_call(
        functools.partial(_dsa_kernel, tq=tq, bk=bk, topk=topk, n_grp=4),
        grid=(batch, nq),
        in_specs=[pl.BlockSpec((tq, nh * DSA_KEY_DIM), lambda b, i: (b * nq + i, 0)),
                  pl.BlockSpec((tq, IDX_HEADS * IDX_DIM), lambda b, i: (b * nq + i, 0)),
                  pl.BlockSpec((tq, LANES), lambda b, i: (b * nq + i, km_blk)),
                  tspec, tspec,
                  pl.BlockSpec((seq, DSA_KEY_DIM), lambda b, i: (b, 0)),
                  pl.BlockSpec((seq, LANES), lambda b, i: (b, 0)),
                  pl.BlockSpec(wuv2.shape, lambda b, i: (0, 0, 0))],
        out_specs=pl.BlockSpec((tq, nh * DSA_V_HEAD), lambda b, i: (b * nq + i, 0)),
        out_shape=jax.ShapeDtypeStruct((batch * seq, nh * DSA_V_HEAD), BF16),
        scratch_shapes=[pltpu.VMEM((tq, seq), BF16),
                        pltpu.VMEM((seq, tq), F32),
                        pltpu.VMEM((seq, tq), jnp.int16),
                        pltpu.VMEM((seq, tq), jnp.int16),
                        pltpu.VMEM((nh * tq, DSA_KEY_DIM), BF16),
                        pltpu.VMEM((nh * tq, LANES), BF16),
                        pltpu.VMEM((nh * tq, LANES), F32),
                        pltpu.VMEM((nh * tq, DSA_KEY_DIM), F32),
                        pltpu.VMEM((nh * tq, bk), F32)],
        compiler_params=_cparams("parallel", "arbitrary"),
        name="dsa_attention",
    )(hq, hr, hr, *tabs, klr, kid, wuv2)


def _rope_tables(positions):
    pos = positions.astype(F32).reshape(-1, 1)

    def freqs(half):
        return ROPE_THETA ** (-jnp.arange(half, dtype=F32) / half)

    fq, fi = freqs(DSA_ROPE // 2), freqs(IDX_DIM // 8)
    f_all = jnp.concatenate([fq, fq, fi, fi, jnp.zeros((LANES - DSA_ROPE - IDX_DIM // 4,), F32)])
    ang = pos * f_all[None, :]
    return jnp.cos(ang), jnp.sin(ang)


def _expand_rope_tables(c, s, lane):
    half_q, half_i = DSA_ROPE // 2, IDX_DIM // 8
    cos_q = jnp.where(lane < DSA_ROPE, c, 1.0)
    sin_q = jnp.where(lane < half_q, -s, jnp.where(lane < DSA_ROPE, s, 0.0))
    l64 = lane & (IDX_DIM - 1)
    ci = jnp.where(lane < IDX_DIM, pltpu.roll(c, IDX_DIM, 1), c)
    si = jnp.where(lane < IDX_DIM, pltpu.roll(s, IDX_DIM, 1), s)
    cos_i = jnp.where(l64 < 2 * half_i, ci, 1.0)
    sin_i = jnp.where(l64 < half_i, -si, jnp.where(l64 < 2 * half_i, si, 0.0))
    return cos_q, sin_q, cos_i, sin_i


def kernel(x, positions, l0_w_in, l0_b_f, l0_w_o, l0_ln1_g, l0_ln1_b, l0_w_up, l0_conv_w, l0_conv_b, l0_w_down, l0_ln2_g, l0_ln2_b, l1_w_in, l1_w_uv, l1_w_o, l1_ln1_g, l1_ln1_b, l1_w_up, l1_conv_w, l1_conv_b, l1_w_down, l1_ln2_g, l1_ln2_b):
    batch, seq, d = x.shape
    m = batch * seq
    xf = x.reshape(m, d).astype(F32)
    xb = xf.astype(BF16)
    t = _plan(batch, seq, l0_w_down.shape[0])

    n_qkv = 3 * (N_HEADS_SB + N_HEADS_FOX) * HEAD_DIM
    h0 = _matmul(xb, l0_w_in[:, :n_qkv].astype(BF16), BF16, t.mm_rows, t.in_cols, "l0_in_proj")
    w_f = jnp.pad(l0_w_in[:, n_qkv:], ((0, 0), (0, LANES - N_HEADS_FOX))).astype(BF16)
    fl = _matmul(xb, w_f, F32, t.mm_rows, LANES, "l0_forget_proj")
    cum = _forget_cumsum(fl, l0_b_f, batch, seq)
    oa = _sb_attention(h0, batch, seq, t.attn_q, t.sb_keys)
    ob = _fox_attention(h0, cum, batch, seq, t.attn_q)
    wo = l0_w_o.astype(BF16)
    na = N_HEADS_SB * HEAD_DIM
    x1f, x1b = _matmul_ln([(oa, wo[:na]), (ob, wo[na:])], xf, l0_ln1_g, l0_ln1_b, t.ln_rows_attn, "l0_out_proj_ln")
    a0 = _ffn_up(x1b, l0_w_up.astype(BF16), l0_conv_w, l0_conv_b, seq, t.ffn_rows, t.ffn_cols, "l0_ffn_up_act")
    x2f, x2b = _matmul_ln([(a0, l0_w_down.astype(BF16))], x1f, l0_ln2_g, l0_ln2_b, t.ln_rows_ffn, "l0_ffn_down_ln")

    c0 = N_HEADS_DSA * DSA_KEY_DIM
    c2 = c0 + DSA_KEY_DIM
    c3 = c2 + IDX_HEADS * IDX_DIM
    n_in = l1_w_in.shape[1]
    hq = _matmul(x2b, l1_w_in[:, :c0].astype(BF16), BF16, t.mm_rows, t.q_cols, "l1_q_proj")
    w_rest = jnp.concatenate([l1_w_in[:, c2:c3], l1_w_in[:, c0:c2], l1_w_in[:, c3:],
                              jnp.zeros((d, LANES - (n_in - c3)), F32)], axis=1).astype(BF16)
    hr = _matmul(x2b, w_rest, F32, t.aux_rows, w_rest.shape[1], "l1_kidx_proj")
    tabs = _rope_tables(positions)
    klr, kid = _dsa_prep(hr, tabs, t.aux_rows)
    wuv = l1_w_uv.astype(BF16)
    zpad = jnp.zeros((DSA_ROPE, DSA_V_HEAD), BF16)
    zblk = jnp.zeros((DSA_KEY_DIM, DSA_V_HEAD), BF16)
    wuv2 = jnp.stack([
        jnp.concatenate([jnp.concatenate([zpad, wuv[2 * p], zblk], axis=0),
                         jnp.concatenate([zblk, zpad, wuv[2 * p + 1]], axis=0)], axis=1)
        for p in range(N_HEADS_DSA // 2)])
    o1 = _dsa(hq, hr, klr, kid, tabs, wuv2, batch, seq, t.dsa_q, t.dsa_keys)
    x3f, x3b = _matmul_ln([(o1, l1_w_o.astype(BF16))], x2f, l1_ln1_g, l1_ln1_b, t.ln_rows_attn, "l1_out_proj_ln")
    a1 = _ffn_up(x3b, l1_w_up.astype(BF16), l1_conv_w, l1_conv_b, seq, t.ffn_rows, t.ffn_cols, "l1_ffn_up_act")
    x4f, _ = _matmul_ln([(a1, l1_w_down.astype(BF16))], x3f, l1_ln2_g, l1_ln2_b, t.ln_rows_ffn, "l1_ffn_down_ln")
    return x4f.reshape(batch, seq, d)
```
